```python
import jax, jax.numpy as jnp
from jax import lax
import numpy as np

D_MODEL = 2048
BATCH = 8
SEQ = 2048
DEPTH = 4

N_MIXERS = 4
RMS_EPS = 1e-6
LN_EPS = 1e-5
N_CONF = (DEPTH + 3) // 4
N_SCONV = (DEPTH + 2) // 4
N_MLSTM = (DEPTH + 1) // 4
N_GLA = DEPTH // 4
CONF_WIDTH = 31
SCONV_WIDTH = 3
MLSTM_HEADS = 8
MLSTM_QK = D_MODEL // 2
MLSTM_V = D_MODEL
MLSTM_CHUNK = 128
GATE_SOFTCAP = 15.0
MLSTM_IN = 2 * MLSTM_QK + 2 * MLSTM_V + 2 * MLSTM_HEADS
GLA_HEADS = 4
GLA_K = D_MODEL // 2
GLA_V = D_MODEL
GLA_RANK = 16
GLA_TEMP = 16.0
GLA_CHUNK = 64
GLA_IN = 2 * GLA_K + 2 * GLA_V + GLA_RANK
D_FF = -(-8 * D_MODEL // (3 * 256)) * 256

kernel_name = "interleaved_hybrid_conv_mlstm_gla_adaln"


def rmsnorm(x, g):
    xf = x.astype(jnp.float32)
    y = xf * lax.rsqrt(jnp.mean(xf * xf, axis=-1, keepdims=True) + RMS_EPS)
    return (y * g.astype(jnp.float32)).astype(x.dtype)


def head_rmsnorm(o, n_heads, g):
    b, t, w = o.shape
    of = o.astype(jnp.float32).reshape(b, t, n_heads, w // n_heads)
    of = of * lax.rsqrt(jnp.mean(of * of, axis=-1, keepdims=True) + RMS_EPS)
    return (of.reshape(b, t, w) * g.astype(jnp.float32)).astype(o.dtype)


def causal_dwconv(u, w):
    width, ch = w.shape
    return lax.conv_general_dilated(u, w.reshape(width, 1, ch), window_strides=(1,),
                                    padding=[(width - 1, 0)],
                                    dimension_numbers=('NWC', 'WIO', 'NWC'),
                                    feature_group_count=ch)


def conformer_conv(h, w_in, w_dw, b_dw, ln_g, ln_b, w_out):
    a, g = jnp.split(h @ w_in, 2, axis=-1)
    u = a * jax.nn.sigmoid(g)
    u = causal_dwconv(u, w_dw) + b_dw
    uf = u.astype(jnp.float32)
    mu = jnp.mean(uf, axis=-1, keepdims=True)
    var = jnp.mean(jnp.square(uf - mu), axis=-1, keepdims=True)
    u = ((uf - mu) * lax.rsqrt(var + LN_EPS) * ln_g + ln_b).astype(h.dtype)
    return jax.nn.silu(u) @ w_out


def short_gated_conv(h, w_in, w_conv, w_out):
    bg, cg, xv = jnp.split(h @ w_in, 3, axis=-1)
    return (bg * causal_dwconv(cg * xv, w_conv)) @ w_out


def mlstm(h, w_in, b_if, norm_g, w_out):
    bsz, t, _ = h.shape
    nh, L = MLSTM_HEADS, MLSTM_CHUNK
    dk, dv, nc = MLSTM_QK // nh, MLSTM_V // nh, t // L
    f32 = jnp.float32
    q, k, v, o, if_pre = jnp.split(h @ w_in, [MLSTM_QK, 2 * MLSTM_QK, 2 * MLSTM_QK + MLSTM_V,
                                              2 * MLSTM_QK + 2 * MLSTM_V], axis=-1)
    if_pre = (if_pre + b_if).astype(f32)
    if_pre = GATE_SOFTCAP * jnp.tanh(if_pre / GATE_SOFTCAP)
    log_i, f_pre = jnp.split(if_pre, 2, axis=-1)
    log_f = jax.nn.log_sigmoid(f_pre)

    def chunks(z, d):
        return z.astype(f32).reshape(bsz, nc, L, nh, d).transpose(1, 0, 3, 2, 4)

    def gchunks(z):
        return z.reshape(bsz, nc, L, nh).transpose(1, 0, 3, 2)

    qc = chunks(q, dk) * (dk ** -0.5)
    kc, vc = chunks(k, dk), chunks(v, dv)
    lic, lfc = gchunks(log_i), gchunks(log_f)
    mask = jnp.tril(jnp.ones((L, L), dtype=bool))

    def step(carry, xs):
        C, n, m = carry
        qj, kj, vj, lij, lfj = xs
        b = jnp.cumsum(lfj, axis=-1)
        a_inter = b + m[..., None]
        d_intra = jnp.where(mask, b[..., :, None] - b[..., None, :] + lij[..., None, :], -jnp.inf)
        m_t = jnp.maximum(a_inter, jnp.max(d_intra, axis=-1))
        s = jnp.einsum('bhtd,bhsd->bhts', qj, kj) * jnp.exp(d_intra - m_t[..., None])
        w_inter = jnp.exp(a_inter - m_t)
        num = (jnp.einsum('bhts,bhsv->bhtv', s, vj)
               + w_inter[..., None] * jnp.einsum('bhtd,bhvd->bhtv', qj, C))
        den = jnp.sum(s, axis=-1) + w_inter * jnp.einsum('bhtd,bhd->bht', qj, n)
        h_out = num / jnp.maximum(jnp.abs(den), jnp.exp(-m_t))[..., None]
        b_last = b[..., -1]
        g = b_last[..., None] - b + lij
        m_new = jnp.maximum(b_last + m, jnp.max(g, axis=-1))
        wk = jnp.exp(g - m_new[..., None])
        decay = jnp.exp(b_last + m - m_new)
        C_new = decay[..., None, None] * C + jnp.einsum('bhsv,bhsd->bhvd', vj * wk[..., None], kj)
        n_new = decay[..., None] * n + jnp.einsum('bhs,bhsd->bhd', wk, kj)
        return (C_new, n_new, m_new), h_out

    init = (jnp.zeros((bsz, nh, dv, dk), f32), jnp.zeros((bsz, nh, dk), f32),
            jnp.zeros((bsz, nh), f32))
    _, hs = lax.scan(step, init, (qc, kc, vc, lic, lfc))
    hs = hs.transpose(1, 0, 3, 2, 4).reshape(bsz, t, MLSTM_V).astype(h.dtype)
    hs = head_rmsnorm(hs, nh, norm_g)
    return (jax.nn.sigmoid(o) * hs) @ w_out


def gla(h, w_in, w_alpha, b_alpha, b_r, norm_g, w_out):
    bsz, t, _ = h.shape
    nh, L = GLA_HEADS, GLA_CHUNK
    dk, dv, nc = GLA_K // nh, GLA_V // nh, t // L
    f32 = jnp.float32
    q, k, v, r, a_low = jnp.split(h @ w_in, [GLA_K, 2 * GLA_K, 2 * GLA_K + GLA_V,
                                             2 * GLA_K + 2 * GLA_V], axis=-1)
    log_alpha = jax.nn.log_sigmoid((a_low @ w_alpha + b_alpha).astype(f32)) / GLA_TEMP

    def chunks(z, d):
        return z.astype(f32).reshape(bsz, nc, L, nh, d).transpose(1, 0, 3, 2, 4)

    qc = chunks(q, dk) * (dk ** -0.5)
    kc, vc, lac = chunks(k, dk), chunks(v, dv), chunks(log_alpha, dk)
    mask = jnp.tril(jnp.ones((L, L), dtype=bool))[:, :, None]

    def step(S, xs):
        qj, kj, vj, laj = xs
        bc = jnp.cumsum(laj, axis=-2)
        o_inter = jnp.einsum('bhtd,bhdv->bhtv', qj * jnp.exp(bc), S)
        rel = jnp.where(mask, bc[:, :, :, None, :] - bc[:, :, None, :, :], -jnp.inf)
        A = jnp.einsum('bhtd,bhsd,bhtsd->bhts', qj, kj, jnp.exp(rel))
        o_intra = jnp.einsum('bhts,bhsv->bhtv', A, vj)
        b_last = bc[:, :, -1]
        k_up = kj * jnp.exp(b_last[:, :, None, :] - bc)
        S_new = jnp.exp(b_last)[..., None] * S + jnp.einsum('bhsd,bhsv->bhdv', k_up, vj)
        return S_new, o_inter + o_intra

    _, os_ = lax.scan(step, jnp.zeros((bsz, nh, dk, dv), f32), (qc, kc, vc, lac))
    o = os_.transpose(1, 0, 3, 2, 4).reshape(bsz, t, GLA_V).astype(h.dtype)
    o = head_rmsnorm(o, nh, norm_g)
    return (o * jax.nn.silu(r + b_r)) @ w_out


def swiglu(h, w_gate_up, w_down):
    gate, up = jnp.split(h @ w_gate_up, 2, axis=-1)
    return (jax.nn.silu(gate) * up) @ w_down


def sandwich(x, mod, g_pre, g_post, mixer):
    shift, scale, gate = jnp.split(mod, 3, axis=-1)
    h = rmsnorm(x, g_pre) * (1.0 + scale[:, None, :]) + shift[:, None, :]
    return x + gate[:, None, :] * rmsnorm(mixer(h), g_post)


def setup_inputs(seed: int = 0) -> dict:
    key = jax.random.key(seed)
    ks = jax.random.split(key, 32)
    D, f32 = D_MODEL, jnp.float32

    def nrm(k, shape, s):
        return jax.random.normal(k, shape, f32) * s

    b_if = jnp.concatenate([nrm(ks[13], (N_MLSTM, MLSTM_HEADS), 0.1),
                            jax.random.uniform(ks[14], (N_MLSTM, MLSTM_HEADS), f32, 3.0, 6.0)],
                           axis=-1)
    return {
        "x": nrm(ks[0], (BATCH, SEQ, D), 1.0),
        "c": nrm(ks[1], (BATCH, D), 1.0),
        "conf_w_in": nrm(ks[2], (N_CONF, D, 2 * D), D ** -0.5),
        "conf_w_dw": nrm(ks[3], (N_CONF, CONF_WIDTH, D), CONF_WIDTH ** -0.5),
        "conf_b_dw": nrm(ks[4], (N_CONF, D), 0.02),
        "conf_ln_g": 1.0 + nrm(ks[5], (N_CONF, D), 0.02),
        "conf_ln_b": nrm(ks[6], (N_CONF, D), 0.02),
        "conf_w_out": nrm(ks[7], (N_CONF, D, D), D ** -0.5),
        "sconv_w_in": nrm(ks[8], (N_SCONV, D, 3 * D), D ** -0.5),
        "sconv_w_conv": nrm(ks[9], (N_SCONV, SCONV_WIDTH, D), SCONV_WIDTH ** -0.5),
        "sconv_w_out": nrm(ks[10], (N_SCONV, D, D), D ** -0.5),
        "mlstm_w_in": nrm(ks[11], (N_MLSTM, D, MLSTM_IN), D ** -0.5),
        "mlstm_b_if": b_if,
        "mlstm_norm_g": 1.0 + nrm(ks[15], (N_MLSTM, MLSTM_V), 0.02),
        "mlstm_w_out": nrm(ks[16], (N_MLSTM, MLSTM_V, D), MLSTM_V ** -0.5),
        "gla_w_in": nrm(ks[17], (N_GLA, D, GLA_IN), D ** -0.5),
        "gla_w_alpha": nrm(ks[18], (N_GLA, GLA_RANK, GLA_K), GLA_RANK ** -0.5),
        "gla_b_alpha": nrm(ks[19], (N_GLA, GLA_K), 0.02),
        "gla_b_r": nrm(ks[20], (N_GLA, GLA_V), 0.02),
        "gla_norm_g": 1.0 + nrm(ks[21], (N_GLA, GLA_V), 0.02),
        "gla_w_out": nrm(ks[22], (N_GLA, GLA_V, D), GLA_V ** -0.5),
        "ada_w": nrm(ks[23], (DEPTH, 2, D, 3 * D), 0.5 * D ** -0.5),
        "ada_b": nrm(ks[24], (DEPTH, 2, 3 * D), 0.02),
        "norm_pre": 1.0 + nrm(ks[25], (DEPTH, 2, D), 0.02),
        "norm_post": 1.0 + nrm(ks[26], (DEPTH, 2, D), 0.02),
        "ffn_w_gate_up": nrm(ks[27], (DEPTH, D, 2 * D_FF), D ** -0.5),
        "ffn_w_down": nrm(ks[28], (DEPTH, D_FF, D), D_FF ** -0.5),
    }


def reference(x, c, conf_w_in, conf_w_dw, conf_b_dw, conf_ln_g, conf_ln_b, conf_w_out,
              sconv_w_in, sconv_w_conv, sconv_w_out,
              mlstm_w_in, mlstm_b_if, mlstm_norm_g, mlstm_w_out,
              gla_w_in, gla_w_alpha, gla_b_alpha, gla_b_r, gla_norm_g, gla_w_out,
              ada_w, ada_b, norm_pre, norm_post, ffn_w_gate_up, ffn_w_down):
    mods = jnp.einsum('bd,lsde->lsbe', jax.nn.silu(c), ada_w) + ada_b[:, :, None, :]
    for i in range(DEPTH):
        kind, j = i % N_MIXERS, i // N_MIXERS
        if kind == 0:
            mixer = lambda h, j=j: conformer_conv(h, conf_w_in[j], conf_w_dw[j], conf_b_dw[j],
                                                  conf_ln_g[j], conf_ln_b[j], conf_w_out[j])
        elif kind == 1:
            mixer = lambda h, j=j: short_gated_conv(h, sconv_w_in[j], sconv_w_conv[j],
                                                    sconv_w_out[j])
        elif kind == 2:
            mixer = lambda h, j=j: mlstm(h, mlstm_w_in[j], mlstm_b_if[j], mlstm_norm_g[j],
                                         mlstm_w_out[j])
        else:
            mixer = lambda h, j=j: gla(h, gla_w_in[j], gla_w_alpha[j], gla_b_alpha[j],
                                       gla_b_r[j], gla_norm_g[j], gla_w_out[j])
        x = sandwich(x, mods[i, 0], norm_pre[i, 0], norm_post[i, 0], mixer)
        x = sandwich(x, mods[i, 1], norm_pre[i, 1], norm_post[i, 1],
                     lambda h, i=i: swiglu(h, ffn_w_gate_up[i], ffn_w_down[i]))
    return x
```

```python
import functools

import jax
import jax.numpy as jnp
from jax import lax
from jax.experimental import pallas as pl
from jax.experimental.pallas import tpu as pltpu

F32 = jnp.float32
BF16 = jnp.bfloat16

RMS_EPS = 1e-6
LN_EPS = 1e-5
MLSTM_HEADS = 8
MLSTM_CHUNK = 128
GATE_SOFTCAP = 15.0
GLA_HEADS = 4
GLA_CHUNK = 64
GLA_TEMP = 16.0
GLA_SUB = 16

LANE = 128
GATE_ROWS = 16
VMEM_LIMIT_BYTES = 56 * 1024 * 1024


def _tile(pref, dim):
    return pref if dim % pref == 0 else dim


def _params(*sem):
    return pltpu.CompilerParams(dimension_semantics=sem, vmem_limit_bytes=VMEM_LIMIT_BYTES)


def _sigmoid(z):
    return 1.0 / (1.0 + jnp.exp(-z))


def _silu(z):
    return z * _sigmoid(z)


def _log_sigmoid(z):
    return jnp.minimum(z, 0.0) - jnp.log1p(jnp.exp(-jnp.abs(z)))


def _dot(a, b):
    return jnp.dot(a, b, preferred_element_type=F32)


def _dot_nt(a, b):
    return lax.dot_general(a, b, (((1,), (1,)), ((), ())), preferred_element_type=F32)


def _dot_tn(a, b):
    return lax.dot_general(a, b, (((0,), (0,)), ((), ())), preferred_element_type=F32)


def _split3(x):
    hi = x.astype(BF16)
    r1 = x - hi.astype(F32)
    mid = r1.astype(BF16)
    lo = (r1 - mid.astype(F32)).astype(BF16)
    return hi, mid, lo


def _tri(n, upper):
    r = lax.broadcasted_iota(jnp.int32, (n, n), 0)
    c = lax.broadcasted_iota(jnp.int32, (n, n), 1)
    return jnp.where((r <= c) if upper else (c <= r), 1.0, 0.0).astype(BF16)


def _cumsum_rows(x):
    tri = _tri(x.shape[0], upper=False)
    hi, mid, lo = _split3(x)
    return _dot(tri, hi) + _dot(tri, mid) + _dot(tri, lo)


def _cumsum_lanes(x):
    tri = _tri(x.shape[1], upper=True)
    hi, mid, lo = _split3(x)
    return _dot(hi, tri) + _dot(mid, tri) + _dot(lo, tri)


def _ada_body(c_ref, w_ref, b_ref, o_ref):
    sc = _silu(c_ref[...]).astype(BF16)
    o_ref[0] = _dot(sc, w_ref[0].astype(BF16)) + b_ref[0]


def _ada_mods(c, ada_w, ada_b):
    depth, two, d, d3 = ada_w.shape
    n_sub = depth * two
    bsz = c.shape[0]
    rows = -(-bsz // 16) * 16
    c_pad = jnp.pad(c, ((0, rows - bsz), (0, 0)))
    tn = _tile(1024, d3)
    out = pl.pallas_call(
        _ada_body,
        grid=(n_sub, d3 // tn),
        in_specs=[pl.BlockSpec((rows, d), lambda l, j: (0, 0)),
                  pl.BlockSpec((1, d, tn), lambda l, j: (l, 0, j)),
                  pl.BlockSpec((1, 1, tn), lambda l, j: (l, 0, j))],
        out_specs=pl.BlockSpec((1, rows, tn), lambda l, j: (l, 0, j)),
        out_shape=jax.ShapeDtypeStruct((n_sub, rows, d3), F32),
        compiler_params=_params("arbitrary", "arbitrary"),
        name="ada_mods",
    )(c_pad, ada_w.reshape(n_sub, d, d3), ada_b.reshape(n_sub, 1, d3))
    return out[:, :bsz, :]


def _inproj_body(*refs, n_parts, n_out, epi, with_small, with_small_t, row_chunk):
    x_ref, shift_ref, scale_ref, gpre_ref = refs[:4]
    pos = 4
    w_refs = refs[pos:pos + n_parts]
    pos += n_parts
    ws_ref = wst_ref = None
    if with_small:
        ws_ref = refs[pos]
        pos += 1
    if with_small_t:
        wst_ref = refs[pos]
        pos += 1
    out_refs = refs[pos:pos + n_out]
    pos += n_out
    gs_ref = gt_ref = None
    if with_small:
        gs_ref = refs[pos]
        pos += 1
    if with_small_t:
        gt_ref = refs[pos]
        pos += 1
    hn_ref = refs[pos]

    j = pl.program_id(1)
    tm = x_ref.shape[0]

    @pl.when(j == 0)
    def _():
        def chunk(r, carry):
            rows = pl.ds(pl.multiple_of(r * row_chunk, row_chunk), row_chunk)
            x = x_ref[rows, :]
            y = x * lax.rsqrt(jnp.mean(x * x, axis=-1, keepdims=True) + RMS_EPS) * gpre_ref[...]
            h = y * (1.0 + scale_ref[0]) + shift_ref[0]
            hn_ref[rows, :] = h.astype(BF16)
            return carry
        lax.fori_loop(0, tm // row_chunk, chunk, 0)
        if with_small:
            gs_ref[...] = _dot(hn_ref[...], ws_ref[...])
        if with_small_t:
            gt_ref[...] = _dot_nt(wst_ref[...], hn_ref[...])

    hn = hn_ref[...]
    accs = [_dot(hn, w_ref[...]) for w_ref in w_refs]
    outs = epi(accs, j)
    for o_ref, o in zip(out_refs, outs):
        o_ref[...] = o.astype(o_ref.dtype)


def _inproj(x2, mod3, g_pre, w, part_offsets, n_col_tiles, epi, out_defs, *, tm, tn, seq_len,
            small=None, small_t=None, name="inproj"):
    n, d = x2.shape
    tiles_per_seq = seq_len // tm
    n_parts = len(part_offsets)
    in_specs = [pl.BlockSpec((tm, d), lambda i, j: (i, 0)),
                pl.BlockSpec((1, 1, d), lambda i, j: (i // tiles_per_seq, 0, 0)),
                pl.BlockSpec((1, 1, d), lambda i, j: (i // tiles_per_seq, 0, 1)),
                pl.BlockSpec((1, d), lambda i, j: (0, 0))]
    args = [x2, mod3, mod3, g_pre.reshape(1, d)]
    for off in part_offsets:
        in_specs.append(pl.BlockSpec((d, tn), lambda i, j, off=off: (0, off + j)))
        args.append(w)
    if small is not None:
        in_specs.append(pl.BlockSpec((d, LANE), lambda i, j: (0, 0)))
        args.append(small)
    if small_t is not None:
        in_specs.append(pl.BlockSpec((GATE_ROWS, d), lambda i, j: (0, 0)))
        args.append(small_t)
    out_specs = [pl.BlockSpec((tm, tn), lambda i, j: (i, j)) for _ in out_defs]
    out_shape = [jax.ShapeDtypeStruct((n, n_col_tiles * tn), dt) for dt in out_defs]
    if small is not None:
        out_specs.append(pl.BlockSpec((tm, LANE), lambda i, j: (i, 0)))
        out_shape.append(jax.ShapeDtypeStruct((n, LANE), F32))
    if small_t is not None:
        out_specs.append(pl.BlockSpec((GATE_ROWS, tm), lambda i, j: (0, i)))
        out_shape.append(jax.ShapeDtypeStruct((GATE_ROWS, n), F32))
    body = functools.partial(_inproj_body, n_parts=n_parts, n_out=len(out_defs), epi=epi,
                             with_small=small is not None, with_small_t=small_t is not None,
                             row_chunk=min(256, tm))
    return pl.pallas_call(
        body,
        grid=(n // tm, n_col_tiles),
        in_specs=in_specs,
        out_specs=out_specs,
        out_shape=out_shape,
        scratch_shapes=[pltpu.VMEM((tm, d), BF16)],
        compiler_params=_params("arbitrary", "arbitrary"),
        name=name,
    )(*args)


def _outproj_body(a_ref, w_ref, x_ref, gate_ref, gpost_ref, o_ref, *, nk, row_chunk):
    k = pl.program_id(1)
    part = _dot(a_ref[...], w_ref[...])
    tm = o_ref.shape[0]

    if nk > 1:
        @pl.when(k == 0)
        def _():
            o_ref[...] = part

        @pl.when(k > 0)
        def _():
            o_ref[...] += part
    else:
        o_ref[...] = part

    @pl.when(k == nk - 1)
    def _():
        def chunk(r, carry):
            rows = pl.ds(pl.multiple_of(r * row_chunk, row_chunk), row_chunk)
            y = o_ref[rows, :]
            yn = y * lax.rsqrt(jnp.mean(y * y, axis=-1, keepdims=True) + RMS_EPS) * gpost_ref[...]
            o_ref[rows, :] = x_ref[rows, :] + gate_ref[0] * yn
            return carry
        lax.fori_loop(0, tm // row_chunk, chunk, 0)


def _outproj(a, w, x2, mod3, g_post, *, tm, tk, seq_len, name="outproj"):
    n, kdim = a.shape
    d = w.shape[1]
    nk = kdim // tk
    tiles_per_seq = seq_len // tm
    body = functools.partial(_outproj_body, nk=nk, row_chunk=min(256, tm))
    return pl.pallas_call(
        body,
        grid=(n // tm, nk),
        in_specs=[pl.BlockSpec((tm, tk), lambda i, k: (i, k)),
                  pl.BlockSpec((tk, d), lambda i, k: (k, 0)),
                  pl.BlockSpec((tm, d), lambda i, k: (i, 0)),
                  pl.BlockSpec((1, 1, d), lambda i, k: (i // tiles_per_seq, 0, 2)),
                  pl.BlockSpec((1, d), lambda i, k: (0, 0))],
        out_specs=pl.BlockSpec((tm, d), lambda i, k: (i, 0)),
        out_shape=jax.ShapeDtypeStruct((n, d), F32),
        compiler_params=_params("arbitrary", "arbitrary"),
        name=name,
    )(a, w, x2, mod3, g_post.reshape(1, d))


def _dwconv_body(*refs, width, halo, col_chunk, row_tile, mode):
    if mode == "conformer":
        cur_ref, halo_ref, w_ref, b_ref, lng_ref, lnb_ref, o_ref, win_ref, y_ref = refs
    else:
        cur_ref, halo_ref, w_ref, gatein_ref, o_ref, win_ref, y_ref = refs
    t = pl.program_id(1)
    tt, d = y_ref.shape
    off = halo - (width - 1)

    win_ref[0:halo, :] = jnp.where(t == 0, 0.0, halo_ref[0])
    win_ref[halo:halo + tt, :] = cur_ref[0]

    def col_body(c, carry):
        cs = pl.ds(pl.multiple_of(c * col_chunk, col_chunk), col_chunk)
        for r in range(tt // row_tile):
            acc = jnp.zeros((row_tile, col_chunk), F32)
            for k in range(width):
                acc = acc + w_ref[k:k + 1, cs] * win_ref[pl.ds(r * row_tile + off + k, row_tile), cs]
            y_ref[pl.ds(r * row_tile, row_tile), cs] = acc
        return carry
    lax.fori_loop(0, d // col_chunk, col_body, 0)

    if mode == "conformer":
        u = y_ref[...] + b_ref[...]
        mu = jnp.mean(u, axis=-1, keepdims=True)
        var = jnp.mean(jnp.square(u - mu), axis=-1, keepdims=True)
        un = (u - mu) * lax.rsqrt(var + LN_EPS) * lng_ref[...] + lnb_ref[...]
        o_ref[0] = _silu(un).astype(o_ref.dtype)
    else:
        o_ref[0] = (gatein_ref[0] * y_ref[...]).astype(o_ref.dtype)


def _dwconv(u3, w, *, mode, extra, tt):
    bsz, seq, d = u3.shape
    width = w.shape[0]
    halo = 8 * (-(-(width - 1) // 8))
    w_pad = jnp.pad(w, ((0, halo - width), (0, 0)))
    tiles = tt // halo
    in_specs = [pl.BlockSpec((1, tt, d), lambda b, t: (b, t, 0)),
                pl.BlockSpec((1, halo, d), lambda b, t: (b, jnp.maximum(t * tiles - 1, 0), 0)),
                pl.BlockSpec((halo, d), lambda b, t: (0, 0))]
    args = [u3, u3, w_pad]
    if mode == "conformer":
        for v in extra:
            in_specs.append(pl.BlockSpec((1, d), lambda b, t: (0, 0)))
            args.append(v.reshape(1, d))
    else:
        in_specs.append(pl.BlockSpec((1, tt, d), lambda b, t: (b, t, 0)))
        args.append(extra[0])
    body = functools.partial(_dwconv_body, width=width, halo=halo, col_chunk=min(256, d),
                             row_tile=min(64, tt), mode=mode)
    return pl.pallas_call(
        body,
        grid=(bsz, seq // tt),
        in_specs=in_specs,
        out_specs=pl.BlockSpec((1, tt, d), lambda b, t: (b, t, 0)),
        out_shape=jax.ShapeDtypeStruct((bsz, seq, d), BF16),
        scratch_shapes=[pltpu.VMEM((halo + tt, d), F32), pltpu.VMEM((tt, d), F32)],
        compiler_params=_params("arbitrary", "arbitrary"),
        name="dwconv_" + mode,
    )(*args)


def _mlstm_body(q_ref, k_ref, v_ref, o_ref, gs_ref, gt_ref, brow_ref, bcol_ref, ng_ref, out_ref,
                ct_ref, n_ref, m_ref, *, heads):
    c = pl.program_id(1)
    L = q_ref.shape[1]
    dk = q_ref.shape[2] // heads
    dv = v_ref.shape[2] // heads

    @pl.when(c == 0)
    def _():
        ct_ref[...] = jnp.zeros_like(ct_ref)
        n_ref[...] = jnp.zeros_like(n_ref)
        m_ref[...] = jnp.zeros_like(m_ref)

    gs = gs_ref[0] + brow_ref[...]
    gs = GATE_SOFTCAP * jnp.tanh(gs / GATE_SOFTCAP)
    gt = gt_ref[...] + bcol_ref[...]
    gt = GATE_SOFTCAP * jnp.tanh(gt / GATE_SOFTCAP)
    cum_c = _cumsum_rows(_log_sigmoid(gs))
    cum_r = _cumsum_lanes(_log_sigmoid(gt))

    row = lax.broadcasted_iota(jnp.int32, (L, L), 0)
    col = lax.broadcasted_iota(jnp.int32, (L, L), 1)
    causal = col <= row

    for h in range(heads):
        qh = q_ref[0, :, h * dk:(h + 1) * dk]
        kh = k_ref[0, :, h * dk:(h + 1) * dk]
        vh = v_ref[0, :, h * dv:(h + 1) * dv]
        li_c = gs[:, h:h + 1]
        li_r = gt[h:h + 1, :]
        b_c = cum_c[:, heads + h:heads + h + 1]
        b_r = cum_r[heads + h:heads + h + 1, :]
        m_prev = m_ref[h:h + 1, 0:1]
        ct = ct_ref[h]
        nvec = n_ref[h:h + 1, :]

        d_intra = jnp.where(causal, b_c - b_r + li_r, -jnp.inf)
        a_inter = b_c + m_prev
        m_t = jnp.maximum(a_inter, jnp.max(d_intra, axis=-1, keepdims=True))
        s = _dot_nt(qh, kh) * jnp.exp(d_intra - m_t)
        w_inter = jnp.exp(a_inter - m_t)
        num = _dot(s.astype(BF16), vh) + w_inter * _dot(qh, ct.astype(BF16))
        qn = jnp.sum(qh.astype(F32) * nvec, axis=-1, keepdims=True)
        den = jnp.sum(s, axis=-1, keepdims=True) + w_inter * qn
        h_out = num / jnp.maximum(jnp.abs(den), jnp.exp(-m_t))

        b_last = b_r[:, L - 1:L]
        g_r = b_last - b_r + li_r
        g_c = b_last - b_c + li_c
        m_new = jnp.maximum(b_last + m_prev, jnp.max(g_r, axis=-1, keepdims=True))
        wk_c = jnp.exp(g_c - m_new)
        decay = jnp.exp(b_last + m_prev - m_new)
        vw = (vh.astype(F32) * wk_c).astype(BF16)
        ct_ref[h] = decay * ct + _dot_tn(kh, vw)
        n_ref[h:h + 1, :] = decay * nvec + jnp.sum(kh.astype(F32) * wk_c, axis=0, keepdims=True)
        m_ref[h:h + 1, :] = jnp.broadcast_to(m_new, (1, m_ref.shape[1]))

        hn = h_out * lax.rsqrt(jnp.mean(h_out * h_out, axis=-1, keepdims=True) + RMS_EPS)
        hn = hn * ng_ref[:, h * dv:(h + 1) * dv]
        og = _sigmoid(o_ref[0, :, h * dv:(h + 1) * dv].astype(F32))
        out_ref[0, :, h * dv:(h + 1) * dv] = (og * hn).astype(out_ref.dtype)


def _mlstm_rec(qkvo3, gs3, gt, b_if, norm_g, *, heads, chunk):
    bsz, seq, _ = qkvo3.shape
    dvt = norm_g.shape[0]
    dkt = dvt // 2
    nc = seq // chunk
    brow = jnp.pad(b_if, (0, LANE - b_if.shape[0])).reshape(1, LANE)
    bcol = jnp.pad(b_if, (0, GATE_ROWS - b_if.shape[0])).reshape(GATE_ROWS, 1)
    body = functools.partial(_mlstm_body, heads=heads)
    return pl.pallas_call(
        body,
        grid=(bsz, nc),
        in_specs=[pl.BlockSpec((1, chunk, dkt), lambda b, c: (b, c, 0)),
                  pl.BlockSpec((1, chunk, dkt), lambda b, c: (b, c, 1)),
                  pl.BlockSpec((1, chunk, dvt), lambda b, c: (b, c, 1)),
                  pl.BlockSpec((1, chunk, dvt), lambda b, c: (b, c, 2)),
                  pl.BlockSpec((1, chunk, LANE), lambda b, c: (b, c, 0)),
                  pl.BlockSpec((GATE_ROWS, chunk), lambda b, c: (0, b * nc + c)),
                  pl.BlockSpec((1, LANE), lambda b, c: (0, 0)),
                  pl.BlockSpec((GATE_ROWS, 1), lambda b, c: (0, 0)),
                  pl.BlockSpec((1, dvt), lambda b, c: (0, 0))],
        out_specs=pl.BlockSpec((1, chunk, dvt), lambda b, c: (b, c, 0)),
        out_shape=jax.ShapeDtypeStruct((bsz, seq, dvt), BF16),
        scratch_shapes=[pltpu.VMEM((heads, dkt // heads, dvt // heads), F32),
                        pltpu.VMEM((heads, dkt // heads), F32),
                        pltpu.VMEM((heads, LANE), F32)],
        compiler_params=_params("arbitrary", "arbitrary"),
        name="mlstm_rec",
    )(qkvo3, qkvo3, qkvo3, qkvo3, gs3, gt, brow, bcol, norm_g.reshape(1, dvt))


def _gla_body(q_ref, k_ref, v_ref, r_ref, al_ref, wa_ref, ba_ref, br_ref, ng_ref, out_ref, st_ref,
              *, heads, sub):
    c = pl.program_id(1)
    L = q_ref.shape[1]
    dk = q_ref.shape[2] // heads
    dv = v_ref.shape[2] // heads

    @pl.when(c == 0)
    def _():
        st_ref[...] = jnp.zeros_like(st_ref)

    z = _dot(al_ref[0].astype(BF16), wa_ref[...]) + ba_ref[...]
    log_alpha = _log_sigmoid(z) / GLA_TEMP
    bc_all = _cumsum_rows(log_alpha)

    row = lax.broadcasted_iota(jnp.int32, (L, L), 0)
    col = lax.broadcasted_iota(jnp.int32, (L, L), 1)
    row_in_sub = row & (sub - 1)
    below_block = col < (row - row_in_sub)

    for h in range(heads):
        qf = q_ref[0, :, h * dk:(h + 1) * dk].astype(F32)
        kf = k_ref[0, :, h * dk:(h + 1) * dk].astype(F32)
        vh = v_ref[0, :, h * dv:(h + 1) * dv]
        bc = bc_all[:, h * dk:(h + 1) * dk]
        st = st_ref[h]

        o = _dot_nt((qf * jnp.exp(bc)).astype(BF16), st.astype(BF16))

        blocks = [jnp.zeros((sub, L), F32)]
        for i in range(1, L // sub):
            ref = bc[i * sub:i * sub + 1, :]
            qi = (qf[i * sub:(i + 1) * sub, :] * jnp.exp(bc[i * sub:(i + 1) * sub, :] - ref)).astype(BF16)
            ki = (kf * jnp.exp(jnp.minimum(ref - bc, 0.0))).astype(BF16)
            blocks.append(_dot_nt(qi, ki))
        a = jnp.where(below_block, jnp.concatenate(blocks, axis=0), 0.0)

        for delta in range(sub):
            ks = kf if delta == 0 else pltpu.roll(kf, delta, 0)
            bs = bc if delta == 0 else pltpu.roll(bc, delta, 0)
            e = jnp.exp(jnp.minimum(bc - bs, 0.0))
            val = jnp.sum(qf * ks * e, axis=-1, keepdims=True)
            a = jnp.where((col == row - delta) & (row_in_sub >= delta), val, a)

        o = o + _dot(a.astype(BF16), vh)

        b_last = bc[L - 1:L, :]
        k_up = (kf * jnp.exp(b_last - bc)).astype(BF16)
        st_ref[h] = st * jnp.exp(b_last) + _dot_tn(vh, k_up)

        on = o * lax.rsqrt(jnp.mean(o * o, axis=-1, keepdims=True) + RMS_EPS)
        on = on * ng_ref[:, h * dv:(h + 1) * dv]
        rg = r_ref[0, :, h * dv:(h + 1) * dv].astype(F32) + br_ref[:, h * dv:(h + 1) * dv]
        out_ref[0, :, h * dv:(h + 1) * dv] = (on * _silu(rg)).astype(out_ref.dtype)


def _gla_rec(qkvr3, al3, w_alpha, b_alpha, b_r, norm_g, *, heads, chunk, sub):
    bsz, seq, _ = qkvr3.shape
    dvt = norm_g.shape[0]
    dkt = dvt // 2
    rank = w_alpha.shape[0]
    wa = jnp.pad(w_alpha, ((0, LANE - rank), (0, 0))).astype(BF16)
    body = functools.partial(_gla_body, heads=heads, sub=sub)
    return pl.pallas_call(
        body,
        grid=(bsz, seq // chunk),
        in_specs=[pl.BlockSpec((1, chunk, dkt), lambda b, c: (b, c, 0)),
                  pl.BlockSpec((1, chunk, dkt), lambda b, c: (b, c, 1)),
                  pl.BlockSpec((1, chunk, dvt), lambda b, c: (b, c, 1)),
                  pl.BlockSpec((1, chunk, dvt), lambda b, c: (b, c, 2)),
                  pl.BlockSpec((1, chunk, LANE), lambda b, c: (b, c, 0)),
                  pl.BlockSpec((LANE, dkt), lambda b, c: (0, 0)),
                  pl.BlockSpec((1, dkt), lambda b, c: (0, 0)),
                  pl.BlockSpec((1, dvt), lambda b, c: (0, 0)),
                  pl.BlockSpec((1, dvt), lambda b, c: (0, 0))],
        out_specs=pl.BlockSpec((1, chunk, dvt), lambda b, c: (b, c, 0)),
        out_shape=jax.ShapeDtypeStruct((bsz, seq, dvt), BF16),
        scratch_shapes=[pltpu.VMEM((heads, dvt // heads, dkt // heads), F32)],
        compiler_params=_params("arbitrary", "arbitrary"),
        name="gla_rec",
    )(qkvr3, qkvr3, qkvr3, qkvr3, al3, wa, b_alpha.reshape(1, dkt), b_r.reshape(1, dvt),
      norm_g.reshape(1, dvt))


def _epi_glu(accs, j):
    a, g = accs
    return [a * _sigmoid(g)]


def _epi_swiglu(accs, j):
    g, u = accs
    return [_silu(g) * u]


def _epi_sconv(accs, j):
    bg, cg, xv = accs
    return [bg, cg * xv]


def _epi_scale_first(accs, j, *, n_first, scale):
    (a,) = accs
    return [a * jnp.where(j < n_first, scale, 1.0)]


def _pad_cols(w, cols):
    return jnp.pad(w, ((0, 0), (0, cols - w.shape[1])))


def kernel(x, c, conf_w_in, conf_w_dw, conf_b_dw, conf_ln_g, conf_ln_b, conf_w_out, sconv_w_in, sconv_w_conv, sconv_w_out, mlstm_w_in, mlstm_b_if, mlstm_norm_g, mlstm_w_out, gla_w_in, gla_w_alpha, gla_b_alpha, gla_b_r, gla_norm_g, gla_w_out, ada_w, ada_b, norm_pre, norm_post, ffn_w_gate_up, ffn_w_down):
    bsz, seq, d = x.shape
    n = bsz * seq
    depth = ada_w.shape[0]
    d_ff = ffn_w_down.shape[1]
    tm_in = _tile(1024, seq)
    tm_out = _tile(512, seq)
    tt = _tile(256, seq)

    mods = _ada_mods(c, ada_w, ada_b)
    x2 = x.reshape(n, d)

    for i in range(depth):
        kind, jj = i % 4, i // 4
        mod3 = mods[2 * i].reshape(bsz, 1, 3 * d)
        g_pre, g_post = norm_pre[i, 0], norm_post[i, 0]
        if kind == 0:
            tn = _tile(512, d)
            (u,) = _inproj(x2, mod3, g_pre, conf_w_in[jj].astype(BF16), [0, d // tn], d // tn, _epi_glu,
                           [F32], tm=tm_in, tn=tn, seq_len=seq, name="conf_in")
            a = _dwconv(u.reshape(bsz, seq, d), conf_w_dw[jj], mode="conformer",
                        extra=(conf_b_dw[jj], conf_ln_g[jj], conf_ln_b[jj]), tt=tt)
            w_out = conf_w_out[jj]
        elif kind == 1:
            tn = _tile(512, d)
            bg, p = _inproj(x2, mod3, g_pre, sconv_w_in[jj].astype(BF16), [0, d // tn, 2 * d // tn],
                            d // tn, _epi_sconv, [F32, F32], tm=tm_in, tn=tn, seq_len=seq, name="sconv_in")
            a = _dwconv(p.reshape(bsz, seq, d), sconv_w_conv[jj], mode="sconv",
                        extra=(bg.reshape(bsz, seq, d),), tt=tt)
            w_out = sconv_w_out[jj]
        elif kind == 2:
            heads = MLSTM_HEADS
            qk, vd = d // 2, d
            w_in = mlstm_w_in[jj]
            w_if = w_in[:, 2 * qk + 2 * vd:]
            tn = _tile(1024, qk)
            epi = functools.partial(_epi_scale_first, n_first=qk // tn, scale=float((qk // heads) ** -0.5))
            qkvo, gs, gt = _inproj(x2, mod3, g_pre, w_in.astype(BF16), [0], (2 * qk + 2 * vd) // tn, epi,
                                   [BF16], tm=tm_in, tn=tn, seq_len=seq,
                                   small=_pad_cols(w_if, LANE).astype(BF16),
                                   small_t=_pad_cols(w_if, GATE_ROWS).T.astype(BF16), name="mlstm_in")
            a = _mlstm_rec(qkvo.reshape(bsz, seq, -1), gs.reshape(bsz, seq, LANE), gt, mlstm_b_if[jj],
                           mlstm_norm_g[jj], heads=heads, chunk=MLSTM_CHUNK)
            w_out = mlstm_w_out[jj]
        else:
            heads = GLA_HEADS
            kd, vd = d // 2, d
            w_in = gla_w_in[jj]
            w_al = w_in[:, 2 * kd + 2 * vd:]
            tn = _tile(1024, kd)
            epi = functools.partial(_epi_scale_first, n_first=kd // tn, scale=float((kd // heads) ** -0.5))
            qkvr, al = _inproj(x2, mod3, g_pre, w_in.astype(BF16), [0], (2 * kd + 2 * vd) // tn, epi,
                               [BF16], tm=tm_in, tn=tn, seq_len=seq,
                               small=_pad_cols(w_al, LANE).astype(BF16), name="gla_in")
            a = _gla_rec(qkvr.reshape(bsz, seq, -1), al.reshape(bsz, seq, LANE), gla_w_alpha[jj],
                         gla_b_alpha[jj], gla_b_r[jj], gla_norm_g[jj], heads=heads, chunk=GLA_CHUNK,
                         sub=GLA_SUB)
            w_out = gla_w_out[jj]
        x2 = _outproj(a.reshape(n, d), w_out.astype(BF16), x2, mod3, g_post, tm=tm_out, tk=d,
                      seq_len=seq, name="mixer_out")

        mod3 = mods[2 * i + 1].reshape(bsz, 1, 3 * d)
        tn = _tile(512, d_ff)
        (act,) = _inproj(x2, mod3, norm_pre[i, 1], ffn_w_gate_up[i].astype(BF16), [0, d_ff // tn],
                         d_ff // tn, _epi_swiglu, [BF16], tm=tm_in, tn=tn, seq_len=seq, name="ffn_in")
        tk = d_ff // 4 if (d_ff // 4) % LANE == 0 else d_ff
        x2 = _outproj(act, ffn_w_down[i].astype(BF16), x2, mod3, norm_post[i, 1], tm=tm_out, tk=tk,
                      seq_len=seq, name="ffn_out")

    return x2.reshape(bsz, seq, d)
```

```python
import functools

import numpy as np
import jax
import jax.numpy as jnp
from jax import lax
from jax.experimental import pallas as pl
from jax.experimental.pallas import tpu as pltpu

F32 = jnp.float32
BF16 = jnp.bfloat16

RMS_EPS = 1e-6
LN_EPS = 1e-5
MLSTM_HEADS = 8
MLSTM_CHUNK = 128
GATE_SOFTCAP = 15.0
GLA_HEADS = 4
GLA_CHUNK = 64
GLA_TEMP = 16.0

LANE = 128
GATE_ROWS = 16
VMEM_LIMIT_BYTES = 56 * 1024 * 1024


def _tile(pref, dim):
    return pref if dim % pref == 0 else dim


def _params(*sem):
    return pltpu.CompilerParams(dimension_semantics=sem, vmem_limit_bytes=VMEM_LIMIT_BYTES)


def _sigmoid(z):
    return 1.0 / (1.0 + jnp.exp(-z))


def _silu(z):
    return z * _sigmoid(z)


def _log_sigmoid(z):
    return jnp.minimum(z, 0.0) - jnp.log1p(jnp.exp(-jnp.abs(z)))


def _dot(a, b):
    return jnp.dot(a, b, preferred_element_type=F32)


def _dot_nt(a, b):
    return lax.dot_general(a, b, (((1,), (1,)), ((), ())), preferred_element_type=F32)


def _dot_tn(a, b):
    return lax.dot_general(a, b, (((0,), (0,)), ((), ())), preferred_element_type=F32)


def _split3(x):
    hi = x.astype(BF16)
    r1 = x - hi.astype(F32)
    mid = r1.astype(BF16)
    lo = (r1 - mid.astype(F32)).astype(BF16)
    return hi, mid, lo


def _tri(n, upper):
    r = lax.broadcasted_iota(jnp.int32, (n, n), 0)
    c = lax.broadcasted_iota(jnp.int32, (n, n), 1)
    return jnp.where((r <= c) if upper else (c <= r), 1.0, 0.0).astype(BF16)


def _cumsum_rows(x):
    tri = _tri(x.shape[0], upper=False)
    hi, mid, lo = _split3(x)
    return _dot(tri, hi) + _dot(tri, mid) + _dot(tri, lo)


def _cumsum_lanes(x):
    tri = _tri(x.shape[1], upper=True)
    hi, mid, lo = _split3(x)
    return _dot(hi, tri) + _dot(mid, tri) + _dot(lo, tri)


def _ada_body(c_ref, w_ref, b_ref, o_ref):
    sc = _silu(c_ref[...]).astype(BF16)
    o_ref[0] = _dot(sc, w_ref[0].astype(BF16)) + b_ref[0]


def _ada_mods(c, ada_w, ada_b):
    depth, two, d, d3 = ada_w.shape
    n_sub = depth * two
    bsz = c.shape[0]
    rows = -(-bsz // 16) * 16
    c_pad = jnp.pad(c, ((0, rows - bsz), (0, 0)))
    tn = _tile(1024, d3)
    out = pl.pallas_call(
        _ada_body,
        grid=(n_sub, d3 // tn),
        in_specs=[pl.BlockSpec((rows, d), lambda l, j: (0, 0)),
                  pl.BlockSpec((1, d, tn), lambda l, j: (l, 0, j)),
                  pl.BlockSpec((1, 1, tn), lambda l, j: (l, 0, j))],
        out_specs=pl.BlockSpec((1, rows, tn), lambda l, j: (l, 0, j)),
        out_shape=jax.ShapeDtypeStruct((n_sub, rows, d3), F32),
        compiler_params=_params("arbitrary", "arbitrary"),
        name="ada_mods",
    )(c_pad, ada_w.reshape(n_sub, d, d3), ada_b.reshape(n_sub, 1, d3))
    return out[:, :bsz, :]


def _inproj_body(*refs, n_parts, n_out, epi, with_small, with_small_t, row_chunk):
    x_ref, shift_ref, scale_ref, gpre_ref = refs[:4]
    pos = 4
    w_refs = refs[pos:pos + n_parts]
    pos += n_parts
    ws_ref = wst_ref = None
    if with_small:
        ws_ref = refs[pos]
        pos += 1
    if with_small_t:
        wst_ref = refs[pos]
        pos += 1
    out_refs = refs[pos:pos + n_out]
    pos += n_out
    gs_ref = gt_ref = None
    if with_small:
        gs_ref = refs[pos]
        pos += 1
    if with_small_t:
        gt_ref = refs[pos]
        pos += 1
    hn_ref = refs[pos]

    j = pl.program_id(1)
    tm = x_ref.shape[0]

    @pl.when(j == 0)
    def _():
        def chunk(r, carry):
            rows = pl.ds(pl.multiple_of(r * row_chunk, row_chunk), row_chunk)
            x = x_ref[rows, :]
            y = x * lax.rsqrt(jnp.mean(x * x, axis=-1, keepdims=True) + RMS_EPS) * gpre_ref[...]
            h = y * (1.0 + scale_ref[0]) + shift_ref[0]
            hn_ref[rows, :] = h.astype(BF16)
            return carry
        lax.fori_loop(0, tm // row_chunk, chunk, 0)
        if with_small:
            gs_ref[...] = _dot(hn_ref[...], ws_ref[...])
        if with_small_t:
            gt_ref[...] = _dot_nt(wst_ref[...], hn_ref[...])

    hn = hn_ref[...]
    accs = [_dot(hn, w_ref[...].astype(BF16)) for w_ref in w_refs]
    outs = epi(accs, j)
    for o_ref, o in zip(out_refs, outs):
        o_ref[...] = o.astype(o_ref.dtype)


def _inproj(x2, mod3, g_pre, w, layer, part_offsets, n_col_tiles, epi, out_defs, *, tm, tn, seq_len,
            small=None, small_t=None, name="inproj"):
    n, d = x2.shape
    tiles_per_seq = seq_len // tm
    n_parts = len(part_offsets)
    in_specs = [pl.BlockSpec((tm, d), lambda i, j: (i, 0)),
                pl.BlockSpec((1, 1, d), lambda i, j: (i // tiles_per_seq, 0, 0)),
                pl.BlockSpec((1, 1, d), lambda i, j: (i // tiles_per_seq, 0, 1)),
                pl.BlockSpec((1, d), lambda i, j: (0, 0))]
    args = [x2, mod3, mod3, g_pre.reshape(1, d)]
    for off in part_offsets:
        in_specs.append(pl.BlockSpec((None, d, tn), lambda i, j, off=off: (layer, 0, off + j)))
        args.append(w)
    if small is not None:
        in_specs.append(pl.BlockSpec((d, LANE), lambda i, j: (0, 0)))
        args.append(small)
    if small_t is not None:
        in_specs.append(pl.BlockSpec((GATE_ROWS, d), lambda i, j: (0, 0)))
        args.append(small_t)
    out_specs = [pl.BlockSpec((tm, tn), lambda i, j: (i, j)) for _ in out_defs]
    out_shape = [jax.ShapeDtypeStruct((n, n_col_tiles * tn), dt) for dt in out_defs]
    if small is not None:
        out_specs.append(pl.BlockSpec((tm, LANE), lambda i, j: (i, 0)))
        out_shape.append(jax.ShapeDtypeStruct((n, LANE), F32))
    if small_t is not None:
        out_specs.append(pl.BlockSpec((GATE_ROWS, tm), lambda i, j: (0, i)))
        out_shape.append(jax.ShapeDtypeStruct((GATE_ROWS, n), F32))
    body = functools.partial(_inproj_body, n_parts=n_parts, n_out=len(out_defs), epi=epi,
                             with_small=small is not None, with_small_t=small_t is not None,
                             row_chunk=min(256, tm))
    return pl.pallas_call(
        body,
        grid=(n // tm, n_col_tiles),
        in_specs=in_specs,
        out_specs=out_specs,
        out_shape=out_shape,
        scratch_shapes=[pltpu.VMEM((tm, d), BF16)],
        compiler_params=_params("arbitrary", "arbitrary"),
        name=name,
    )(*args)


def _outproj_body(a_ref, w_ref, x_ref, gate_ref, gpost_ref, o_ref, *, row_chunk):
    o_ref[...] = _dot(a_ref[...], w_ref[...])
    tm = o_ref.shape[0]

    def chunk(r, carry):
        rows = pl.ds(pl.multiple_of(r * row_chunk, row_chunk), row_chunk)
        y = o_ref[rows, :]
        yn = y * lax.rsqrt(jnp.mean(y * y, axis=-1, keepdims=True) + RMS_EPS) * gpost_ref[...]
        o_ref[rows, :] = x_ref[rows, :] + gate_ref[0] * yn
        return carry
    lax.fori_loop(0, tm // row_chunk, chunk, 0)


def _outproj(a, w, x2, mod3, g_post, *, tm, seq_len, name="outproj"):
    n, kdim = a.shape
    d = w.shape[1]
    tiles_per_seq = seq_len // tm
    body = functools.partial(_outproj_body, row_chunk=min(256, tm))
    return pl.pallas_call(
        body,
        grid=(n // tm,),
        in_specs=[pl.BlockSpec((tm, kdim), lambda i: (i, 0)),
                  pl.BlockSpec((kdim, d), lambda i: (0, 0), pipeline_mode=pl.Buffered(1)),
                  pl.BlockSpec((tm, d), lambda i: (i, 0)),
                  pl.BlockSpec((1, 1, d), lambda i: (i // tiles_per_seq, 0, 2)),
                  pl.BlockSpec((1, d), lambda i: (0, 0))],
        out_specs=pl.BlockSpec((tm, d), lambda i: (i, 0)),
        out_shape=jax.ShapeDtypeStruct((n, d), F32),
        compiler_params=_params("arbitrary"),
        name=name,
    )(a, w, x2, mod3, g_post.reshape(1, d))


def _dwconv_body(*refs, width, halo, col_chunk, row_tile, mode):
    if mode == "conformer":
        cur_ref, halo_ref, w_ref, b_ref, lng_ref, lnb_ref, o_ref, win_ref, sh_ref, y_ref = refs
    else:
        cur_ref, halo_ref, w_ref, gatein_ref, o_ref, win_ref, sh_ref, y_ref = refs
    t = pl.program_id(1)
    tt, d = y_ref.shape
    off = halo - (width - 1)
    sh_rows = sh_ref.shape[1]

    win_ref[0:halo, :] = jnp.where(t == 0, 0.0, halo_ref[0])
    win_ref[halo:halo + tt, :] = cur_ref[0]

    def col_body(c, carry):
        cs = pl.ds(pl.multiple_of(c * col_chunk, col_chunk), col_chunk)
        sh_ref[0] = win_ref[:, cs]
        for s in range(1, 8):
            sh_ref[s, 0:sh_rows - 8, :] = win_ref[pl.ds(s, sh_rows - 8), cs]
        for r in range(tt // row_tile):
            acc = jnp.zeros((row_tile, col_chunk), F32)
            for k in range(width):
                phase = (off + k) % 8
                base = off + k - phase
                acc = acc + w_ref[k:k + 1, cs] * sh_ref[phase, pl.ds(r * row_tile + base, row_tile), :]
            y_ref[pl.ds(r * row_tile, row_tile), cs] = acc
        return carry
    lax.fori_loop(0, d // col_chunk, col_body, 0)

    def tail(r, carry):
        rows = pl.ds(pl.multiple_of(r * 16, 16), 16)
        if mode == "conformer":
            u = y_ref[rows, :] + b_ref[...]
            mu = jnp.mean(u, axis=-1, keepdims=True)
            var = jnp.mean(jnp.square(u - mu), axis=-1, keepdims=True)
            un = (u - mu) * lax.rsqrt(var + LN_EPS) * lng_ref[...] + lnb_ref[...]
            o_ref[0, rows, :] = _silu(un).astype(o_ref.dtype)
        else:
            o_ref[0, rows, :] = (gatein_ref[0, rows, :] * y_ref[rows, :]).astype(o_ref.dtype)
        return carry
    lax.fori_loop(0, tt // 16, tail, 0, unroll=4)


def _dwconv(u3, w, *, mode, extra, tt):
    bsz, seq, d = u3.shape
    width = w.shape[0]
    halo = 8 * (-(-(width - 1) // 8))
    w_pad = jnp.pad(w, ((0, halo - width), (0, 0)))
    tiles = tt // halo
    in_specs = [pl.BlockSpec((1, tt, d), lambda b, t: (b, t, 0)),
                pl.BlockSpec((1, halo, d), lambda b, t: (b, jnp.maximum(t * tiles - 1, 0), 0)),
                pl.BlockSpec((halo, d), lambda b, t: (0, 0))]
    args = [u3, u3, w_pad]
    if mode == "conformer":
        for v in extra:
            in_specs.append(pl.BlockSpec((1, d), lambda b, t: (0, 0)))
            args.append(v.reshape(1, d))
    else:
        in_specs.append(pl.BlockSpec((1, tt, d), lambda b, t: (b, t, 0)))
        args.append(extra[0])
    col_chunk = min(256, d)
    body = functools.partial(_dwconv_body, width=width, halo=halo, col_chunk=col_chunk,
                             row_tile=min(64, tt), mode=mode)
    return pl.pallas_call(
        body,
        grid=(bsz, seq // tt),
        in_specs=in_specs,
        out_specs=pl.BlockSpec((1, tt, d), lambda b, t: (b, t, 0)),
        out_shape=jax.ShapeDtypeStruct((bsz, seq, d), BF16),
        scratch_shapes=[pltpu.VMEM((halo + tt, d), F32),
                        pltpu.VMEM((8, halo + tt, col_chunk), F32),
                        pltpu.VMEM((tt, d), F32)],
        compiler_params=_params("arbitrary", "arbitrary"),
        name="dwconv_" + mode,
    )(*args)


def _split3_f32(x):
    hi = x.astype(BF16).astype(F32)
    r1 = x - hi
    mid = r1.astype(BF16).astype(F32)
    lo = (r1 - mid).astype(BF16).astype(F32)
    return hi, mid, lo


def _mlstm_body(q_ref, k_ref, v_ref, o_ref, gt_ref, bcol_ref, ng_ref, out_ref, c_ref, m_ref, *, heads):
    c = pl.program_id(1)
    L = q_ref.shape[1]
    dk = q_ref.shape[2] // heads
    dv = v_ref.shape[2] // heads
    rep = dv // L

    @pl.when(c == 0)
    def _():
        c_ref[...] = jnp.zeros_like(c_ref)
        m_ref[...] = jnp.zeros_like(m_ref)

    g = gt_ref[...] + bcol_ref[...]
    g = GATE_SOFTCAP * jnp.tanh(g / GATE_SOFTCAP)
    cum = _cumsum_lanes(_log_sigmoid(g))
    li = g[0:heads]
    b = cum[heads:2 * heads]
    r = li - b
    lane = lax.broadcasted_iota(jnp.int32, (heads, L), 1)
    pm = r
    shift = 1
    while shift < L:
        pm = jnp.maximum(pm, jnp.where(lane >= shift, pltpu.roll(pm, shift, 1), -jnp.inf))
        shift *= 2
    m_prev = m_ref[...]
    u = jnp.maximum(m_prev, pm)
    u_last = jnp.broadcast_to(u[:, L - 1:L], (heads, L))
    b_last = jnp.broadcast_to(b[:, L - 1:L], (heads, L))
    m_ref[...] = b_last + u_last
    decay = jnp.exp(m_prev - u_last)
    lhs_terms = _split3_f32(-u) + _split3_f32(m_prev - u) + _split3_f32(-(b + u)) + _split3_f32(r - u_last)
    lhs_rows = (0, 1, 2, 6, 7, 8, 9, 10, 11, 12, 13, 14)
    r_terms = _split3_f32(r)

    rid = lax.broadcasted_iota(jnp.int32, (GATE_ROWS, L), 0)
    lhs_const = jnp.where((rid >= 3) & (rid < 6), 1.0, 0.0)
    rhs_blocks = [jnp.where((rid >= lo) & (rid < lo + 3), 1.0, 0.0).astype(BF16) for lo in (6, 9, 12)]
    rhs0_const = jnp.where(rid < 3, 1.0, 0.0)

    row = lax.broadcasted_iota(jnp.int32, (L, L), 0)
    col = lax.broadcasted_iota(jnp.int32, (L, L), 1)
    causal = col <= row
    ones_ll = jnp.ones((L, L), BF16)
    ones_dv = jnp.ones((dv, L), BF16)

    H = range(heads)
    qs = [q_ref[0, :, h * dk:(h + 1) * dk] for h in H]
    ks = [k_ref[0, :, h * dk:(h + 1) * dk] for h in H]
    vs = [v_ref[0, :, h * dv:(h + 1) * dv] for h in H]
    caugs = [c_ref[h] for h in H]

    def operands(h):
        lhs = lhs_const
        for i, term in zip(lhs_rows, lhs_terms):
            lhs = jnp.where(rid == i, term[h:h + 1, :], lhs)
        rhs0 = rhs0_const
        for i, term in zip((3, 4, 5), r_terms):
            rhs0 = jnp.where(rid == i, term[h:h + 1, :], rhs0)
        return lhs.astype(BF16), jnp.concatenate([rhs0.astype(BF16)] + rhs_blocks, axis=1)

    ops = [operands(h) for h in H]
    zs = [_dot_tn(lhs, rhs) for lhs, rhs in ops]
    qks = [_dot_nt(qs[h], ks[h]) for h in H]
    qcs = [_dot(qs[h], caugs[h].astype(BF16)) for h in H]
    ps = [jnp.exp(jnp.where(causal, z[:, 0:L], -jnp.inf)) for z in zs]
    ss = [(qks[h] * ps[h]).astype(BF16) for h in H]
    svs = [_dot(ss[h], jnp.concatenate([vs[h], ones_ll], axis=1)) for h in H]
    nds = [svs[h] + jnp.concatenate([jnp.exp(zs[h][:, L:2 * L])] * (rep + 1), axis=1) * qcs[h] for h in H]
    scales = [1.0 / jnp.maximum(jnp.abs(nds[h][:, dv:]), jnp.exp(zs[h][:, 2 * L:3 * L])) for h in H]
    houts = [nds[h][:, :dv] * jnp.concatenate([scales[h]] * rep, axis=1) for h in H]

    for h in H:
        wk = jnp.exp(zs[h][:, 3 * L:4 * L])
        vw = jnp.concatenate([vs[h].astype(F32) * jnp.concatenate([wk] * rep, axis=1), wk], axis=1)
        dec = jnp.concatenate([jnp.broadcast_to(decay[h:h + 1, :], (dk, L))] * (rep + 1), axis=1)
        c_ref[h] = dec * caugs[h] + _dot_tn(ks[h], vw.astype(BF16))

    def mean_sq(x):
        xx = x * x
        hi = xx.astype(BF16)
        lo = (xx - hi.astype(F32)).astype(BF16)
        return (_dot(hi, ones_dv) + _dot(lo, ones_dv)) * (1.0 / dv)

    mss = [mean_sq(houts[h]) for h in H]
    for h in H:
        hn = houts[h] * jnp.concatenate([lax.rsqrt(mss[h] + RMS_EPS)] * rep, axis=1)
        hn = hn * ng_ref[:, h * dv:(h + 1) * dv]
        og = _sigmoid(o_ref[0, :, h * dv:(h + 1) * dv].astype(F32))
        out_ref[0, :, h * dv:(h + 1) * dv] = (og * hn).astype(out_ref.dtype)


def _mlstm_rec(qkvo3, gt, b_if, norm_g, *, heads, chunk):
    bsz, seq, _ = qkvo3.shape
    dvt = norm_g.shape[0]
    dkt = dvt // 2
    nc = seq // chunk
    assert (dvt // heads) % chunk == 0 and 2 * heads <= GATE_ROWS
    bcol = jnp.pad(b_if, (0, GATE_ROWS - b_if.shape[0])).reshape(GATE_ROWS, 1)
    body = functools.partial(_mlstm_body, heads=heads)
    return pl.pallas_call(
        body,
        grid=(bsz, nc),
        in_specs=[pl.BlockSpec((1, chunk, dkt), lambda b, c: (b, c, 0)),
                  pl.BlockSpec((1, chunk, dkt), lambda b, c: (b, c, 1)),
                  pl.BlockSpec((1, chunk, dvt), lambda b, c: (b, c, 1)),
                  pl.BlockSpec((1, chunk, dvt), lambda b, c: (b, c, 2)),
                  pl.BlockSpec((GATE_ROWS, chunk), lambda b, c: (0, b * nc + c)),
                  pl.BlockSpec((GATE_ROWS, 1), lambda b, c: (0, 0)),
                  pl.BlockSpec((1, dvt), lambda b, c: (0, 0))],
        out_specs=pl.BlockSpec((1, chunk, dvt), lambda b, c: (b, c, 0)),
        out_shape=jax.ShapeDtypeStruct((bsz, seq, dvt), BF16),
        scratch_shapes=[pltpu.VMEM((heads, dkt // heads, dvt // heads + chunk), F32),
                        pltpu.VMEM((heads, chunk), F32)],
        compiler_params=_params("arbitrary", "arbitrary"),
        name="mlstm_rec",
    )(qkvo3, qkvo3, qkvo3, qkvo3, gt, bcol, norm_g.reshape(1, dvt))


def _gla_level_masks(chunk):
    t = np.arange(chunk)[:, None]
    s = np.arange(chunk)[None, :]
    masks = []
    b = chunk // 2
    while b >= 1:
        masks.append((t // (2 * b) == s // (2 * b)) & ((t % (2 * b)) >= b) & ((s % (2 * b)) < b))
        b //= 2
    return jnp.asarray(np.stack(masks), dtype=F32)


def _mid_rows(bc, b, row):
    L, dk = bc.shape
    if b >= 4:
        return jnp.concatenate([jnp.broadcast_to(bc[m:m + 1, :], (2 * b, dk)) for m in range(b, L, 2 * b)],
                               axis=0)
    up1 = pltpu.roll(bc, L - 1, 0)
    if b == 1:
        return jnp.where((row & 1) == 0, up1, bc)
    up2 = pltpu.roll(bc, L - 2, 0)
    down1 = pltpu.roll(bc, 1, 0)
    r4 = row & 3
    return jnp.where(r4 == 0, up2, jnp.where(r4 == 1, up1, jnp.where(r4 == 2, bc, down1)))


def _gla_body(q_ref, k_ref, v_ref, r_ref, al_ref, wa_ref, ba_ref, br_ref, ng_ref, mask_ref,
              out_ref, st_ref, *, heads):
    c = pl.program_id(1)
    L = q_ref.shape[1]
    dk = q_ref.shape[2] // heads
    dv = v_ref.shape[2] // heads
    n_levels = mask_ref.shape[0]

    @pl.when(c == 0)
    def _():
        st_ref[...] = jnp.zeros_like(st_ref)

    z = _dot(al_ref[0].astype(BF16), wa_ref[...]) + ba_ref[...]
    log_alpha = _log_sigmoid(z) / GLA_TEMP
    bc_all = _cumsum_rows(log_alpha)

    eye = (lax.broadcasted_iota(jnp.int32, (L, L), 0) == lax.broadcasted_iota(jnp.int32, (L, L), 1))
    row = lax.broadcasted_iota(jnp.int32, (L, dk), 0)

    H = range(heads)
    qfs = [q_ref[0, :, h * dk:(h + 1) * dk].astype(F32) for h in H]
    kfs = [k_ref[0, :, h * dk:(h + 1) * dk].astype(F32) for h in H]
    vs = [v_ref[0, :, h * dv:(h + 1) * dv] for h in H]
    bcs = [bc_all[:, h * dk:(h + 1) * dk] for h in H]
    sts = [st_ref[h] for h in H]

    os_ = [_dot_nt((qfs[h] * jnp.exp(bcs[h])).astype(BF16), sts[h].astype(BF16)) for h in H]

    accs = [jnp.where(eye, jnp.sum(qfs[h] * kfs[h], axis=-1, keepdims=True), 0.0) for h in H]
    for i in range(n_levels):
        b = L >> (i + 1)
        upper = (row & (2 * b - 1)) >= b
        cls = [(jnp.where(upper, qfs[h], kfs[h])
                * jnp.exp(-jnp.abs(bcs[h] - _mid_rows(bcs[h], b, row)))).astype(BF16) for h in H]
        accs = [accs[h] + mask_ref[i] * _dot_nt(cls[h], cls[h]) for h in H]
    os_ = [os_[h] + _dot(accs[h].astype(BF16), vs[h]) for h in H]

    for h in H:
        b_last = bcs[h][L - 1:L, :]
        k_up = (kfs[h] * jnp.exp(jnp.minimum(b_last - bcs[h], 0.0))).astype(BF16)
        st_ref[h] = sts[h] * jnp.exp(b_last) + _dot_tn(vs[h], k_up)

    for h in H:
        o = os_[h]
        on = o * lax.rsqrt(jnp.mean(o * o, axis=-1, keepdims=True) + RMS_EPS)
        on = on * ng_ref[:, h * dv:(h + 1) * dv]
        rg = r_ref[0, :, h * dv:(h + 1) * dv].astype(F32) + br_ref[:, h * dv:(h + 1) * dv]
        out_ref[0, :, h * dv:(h + 1) * dv] = (on * _silu(rg)).astype(out_ref.dtype)


def _gla_rec(qkvr3, al3, w_alpha, b_alpha, b_r, norm_g, *, heads, chunk):
    bsz, seq, _ = qkvr3.shape
    dvt = norm_g.shape[0]
    dkt = dvt // 2
    rank = w_alpha.shape[0]
    wa = jnp.pad(w_alpha, ((0, LANE - rank), (0, 0))).astype(BF16)
    masks = _gla_level_masks(chunk)
    body = functools.partial(_gla_body, heads=heads)
    return pl.pallas_call(
        body,
        grid=(bsz, seq // chunk),
        in_specs=[pl.BlockSpec((1, chunk, dkt), lambda b, c: (b, c, 0)),
                  pl.BlockSpec((1, chunk, dkt), lambda b, c: (b, c, 1)),
                  pl.BlockSpec((1, chunk, dvt), lambda b, c: (b, c, 1)),
                  pl.BlockSpec((1, chunk, dvt), lambda b, c: (b, c, 2)),
                  pl.BlockSpec((1, chunk, LANE), lambda b, c: (b, c, 0)),
                  pl.BlockSpec((LANE, dkt), lambda b, c: (0, 0)),
                  pl.BlockSpec((1, dkt), lambda b, c: (0, 0)),
                  pl.BlockSpec((1, dvt), lambda b, c: (0, 0)),
                  pl.BlockSpec((1, dvt), lambda b, c: (0, 0)),
                  pl.BlockSpec(masks.shape, lambda b, c: (0, 0, 0))],
        out_specs=pl.BlockSpec((1, chunk, dvt), lambda b, c: (b, c, 0)),
        out_shape=jax.ShapeDtypeStruct((bsz, seq, dvt), BF16),
        scratch_shapes=[pltpu.VMEM((heads, dvt // heads, dkt // heads), F32)],
        compiler_params=_params("arbitrary", "arbitrary"),
        name="gla_rec",
    )(qkvr3, qkvr3, qkvr3, qkvr3, al3, wa, b_alpha.reshape(1, dkt), b_r.reshape(1, dvt),
      norm_g.reshape(1, dvt), masks)


def _epi_glu(accs, j):
    a, g = accs
    return [a * _sigmoid(g)]


def _epi_swiglu(accs, j):
    g, u = accs
    return [_silu(g) * u]


def _epi_sconv(accs, j):
    bg, cg, xv = accs
    return [bg, cg * xv]


def _epi_scale_first(accs, j, *, n_first, scale):
    (a,) = accs
    return [a * jnp.where(j < n_first, scale, 1.0)]


def _pad_cols(w, cols):
    return jnp.pad(w, ((0, 0), (0, cols - w.shape[1])))


def kernel(x, c, conf_w_in, conf_w_dw, conf_b_dw, conf_ln_g, conf_ln_b, conf_w_out, sconv_w_in, sconv_w_conv, sconv_w_out, mlstm_w_in, mlstm_b_if, mlstm_norm_g, mlstm_w_out, gla_w_in, gla_w_alpha, gla_b_alpha, gla_b_r, gla_norm_g, gla_w_out, ada_w, ada_b, norm_pre, norm_post, ffn_w_gate_up, ffn_w_down):
    bsz, seq, d = x.shape
    n = bsz * seq
    depth = ada_w.shape[0]
    d_ff = ffn_w_down.shape[1]
    tm_in = _tile(1024, seq)
    tm_out = _tile(512, seq)
    tt = _tile(256, seq)

    mods = _ada_mods(c, ada_w, ada_b)
    x2 = x.reshape(n, d)

    for i in range(depth):
        kind, jj = i % 4, i // 4
        mod3 = mods[2 * i].reshape(bsz, 1, 3 * d)
        g_pre, g_post = norm_pre[i, 0], norm_post[i, 0]
        if kind == 0:
            tn = _tile(512, d)
            (u,) = _inproj(x2, mod3, g_pre, conf_w_in, jj, [0, d // tn], d // tn, _epi_glu,
                           [F32], tm=tm_in, tn=tn, seq_len=seq, name="conf_in")
            a = _dwconv(u.reshape(bsz, seq, d), conf_w_dw[jj], mode="conformer",
                        extra=(conf_b_dw[jj], conf_ln_g[jj], conf_ln_b[jj]), tt=tt)
            w_out = conf_w_out[jj]
        elif kind == 1:
            tn = _tile(512, d)
            bg, p = _inproj(x2, mod3, g_pre, sconv_w_in.astype(BF16), jj, [0, d // tn, 2 * d // tn],
                            d // tn, _epi_sconv, [F32, F32], tm=tm_in, tn=tn, seq_len=seq, name="sconv_in")
            a = _dwconv(p.reshape(bsz, seq, d), sconv_w_conv[jj], mode="sconv",
                        extra=(bg.reshape(bsz, seq, d),), tt=tt)
            w_out = sconv_w_out[jj]
        elif kind == 2:
            heads = MLSTM_HEADS
            qk, vd = d // 2, d
            w_in = mlstm_w_in[jj]
            w_if = w_in[:, 2 * qk + 2 * vd:]
            tn = _tile(1024, qk)
            epi = functools.partial(_epi_scale_first, n_first=qk // tn, scale=float((qk // heads) ** -0.5))
            qkvo, gt = _inproj(x2, mod3, g_pre, mlstm_w_in, jj, [0], (2 * qk + 2 * vd) // tn, epi,
                               [BF16], tm=tm_in, tn=tn, seq_len=seq,
                               small_t=_pad_cols(w_if, GATE_ROWS).T.astype(BF16), name="mlstm_in")
            a = _mlstm_rec(qkvo.reshape(bsz, seq, -1), gt, mlstm_b_if[jj], mlstm_norm_g[jj], heads=heads,
                           chunk=MLSTM_CHUNK)
            w_out = mlstm_w_out[jj]
        else:
            heads = GLA_HEADS
            kd, vd = d // 2, d
            w_in = gla_w_in[jj]
            w_al = w_in[:, 2 * kd + 2 * vd:]
            tn = _tile(1024, kd)
            epi = functools.partial(_epi_scale_first, n_first=kd // tn, scale=float((kd // heads) ** -0.5))
            qkvr, al = _inproj(x2, mod3, g_pre, gla_w_in, jj, [0], (2 * kd + 2 * vd) // tn, epi,
                               [BF16], tm=tm_in, tn=tn, seq_len=seq,
                               small=_pad_cols(w_al, LANE).astype(BF16), name="gla_in")
            a = _gla_rec(qkvr.reshape(bsz, seq, -1), al.reshape(bsz, seq, LANE), gla_w_alpha[jj],
                         gla_b_alpha[jj], gla_b_r[jj], gla_norm_g[jj], heads=heads, chunk=GLA_CHUNK)
            w_out = gla_w_out[jj]
        x2 = _outproj(a.reshape(n, d), w_out.astype(BF16), x2, mod3, g_post, tm=tm_out, seq_len=seq,
                      name="mixer_out")

        mod3 = mods[2 * i + 1].reshape(bsz, 1, 3 * d)
        tn = _tile(512, d_ff)
        (act,) = _inproj(x2, mod3, norm_pre[i, 1], ffn_w_gate_up, i, [0, d_ff // tn],
                         d_ff // tn, _epi_swiglu, [BF16], tm=tm_in, tn=tn, seq_len=seq, name="ffn_in")
        x2 = _outproj(act, ffn_w_down[i].astype(BF16), x2, mod3, norm_post[i, 1], tm=tm_out, seq_len=seq,
                      name="ffn_out")

    return x2.reshape(bsz, seq, d)
```

```python
import functools

import numpy as np
import jax
import jax.numpy as jnp
from jax import lax
from jax.experimental import pallas as pl
from jax.experimental.pallas import tpu as pltpu

F32 = jnp.float32
BF16 = jnp.bfloat16

RMS_EPS = 1e-6
LN_EPS = 1e-5
MLSTM_HEADS = 8
MLSTM_CHUNK = 128
GATE_SOFTCAP = 15.0
GLA_HEADS = 4
GLA_CHUNK = 64
GLA_TEMP = 16.0

LANE = 128
GATE_ROWS = 16
VMEM_LIMIT_BYTES = 56 * 1024 * 1024


def _tile(pref, dim):
    return pref if dim % pref == 0 else dim


def _params(*sem):
    return pltpu.CompilerParams(dimension_semantics=sem, vmem_limit_bytes=VMEM_LIMIT_BYTES)


def _sigmoid(z):
    return 1.0 / (1.0 + jnp.exp(-z))


def _silu(z):
    return z * _sigmoid(z)


def _log_sigmoid(z):
    return jnp.minimum(z, 0.0) - jnp.log1p(jnp.exp(-jnp.abs(z)))


def _dot(a, b):
    return jnp.dot(a, b, preferred_element_type=F32)


def _dot_nt(a, b):
    return lax.dot_general(a, b, (((1,), (1,)), ((), ())), preferred_element_type=F32)


def _dot_tn(a, b):
    return lax.dot_general(a, b, (((0,), (0,)), ((), ())), preferred_element_type=F32)


def _split3(x):
    hi = x.astype(BF16)
    r1 = x - hi.astype(F32)
    mid = r1.astype(BF16)
    lo = (r1 - mid.astype(F32)).astype(BF16)
    return hi, mid, lo


def _tri(n, upper):
    r = lax.broadcasted_iota(jnp.int32, (n, n), 0)
    c = lax.broadcasted_iota(jnp.int32, (n, n), 1)
    return jnp.where((r <= c) if upper else (c <= r), 1.0, 0.0).astype(BF16)


def _cumsum_rows(x):
    tri = _tri(x.shape[0], upper=False)
    hi, mid, lo = _split3(x)
    return _dot(tri, hi) + _dot(tri, mid) + _dot(tri, lo)


def _cumsum_lanes(x):
    tri = _tri(x.shape[1], upper=True)
    hi, mid, lo = _split3(x)
    return _dot(hi, tri) + _dot(mid, tri) + _dot(lo, tri)


def _ada_body(c_ref, w_ref, b_ref, o_ref):
    sc = _silu(c_ref[...]).astype(BF16)
    o_ref[0] = _dot(sc, w_ref[0].astype(BF16)) + b_ref[0]


def _ada_mods(c, ada_w, ada_b):
    depth, two, d, d3 = ada_w.shape
    n_sub = depth * two
    bsz = c.shape[0]
    rows = -(-bsz // 16) * 16
    c_pad = jnp.pad(c, ((0, rows - bsz), (0, 0)))
    tn = _tile(1024, d3)
    out = pl.pallas_call(
        _ada_body,
        grid=(n_sub, d3 // tn),
        in_specs=[pl.BlockSpec((rows, d), lambda l, j: (0, 0)),
                  pl.BlockSpec((1, d, tn), lambda l, j: (l, 0, j)),
                  pl.BlockSpec((1, 1, tn), lambda l, j: (l, 0, j))],
        out_specs=pl.BlockSpec((1, rows, tn), lambda l, j: (l, 0, j)),
        out_shape=jax.ShapeDtypeStruct((n_sub, rows, d3), F32),
        compiler_params=_params("arbitrary", "arbitrary"),
        name="ada_mods",
    )(c_pad, ada_w.reshape(n_sub, d, d3), ada_b.reshape(n_sub, 1, d3))
    return out[:, :bsz, :]


def _inproj_body(*refs, n_parts, n_out, epi, with_small_t, row_chunk):
    x_ref, shift_ref, scale_ref, gpre_ref = refs[:4]
    w_refs = refs[4:4 + n_parts]
    pos = 4 + n_parts
    wst_ref = gt_ref = None
    if with_small_t:
        wst_ref = refs[pos]
        pos += 1
    out_refs = refs[pos:pos + n_out]
    pos += n_out
    if with_small_t:
        gt_ref = refs[pos]
        pos += 1
    hn_ref = refs[pos]

    j = pl.program_id(1)
    tm = x_ref.shape[0]

    @pl.when(j == 0)
    def _():
        gain = gpre_ref[...] * (1.0 + scale_ref[0])
        shift = shift_ref[0]

        def chunk(r, carry):
            rows = pl.ds(pl.multiple_of(r * row_chunk, row_chunk), row_chunk)
            x = x_ref[rows, :]
            h = x * lax.rsqrt(jnp.mean(x * x, axis=-1, keepdims=True) + RMS_EPS) * gain + shift
            hn_ref[rows, :] = h.astype(BF16)
            return carry
        lax.fori_loop(0, tm // row_chunk, chunk, 0, unroll=4)
        if with_small_t:
            gt_ref[...] = _dot_nt(wst_ref[...], hn_ref[...])

    hn = hn_ref[...]
    accs = [_dot(hn, w_ref[...].astype(BF16)) for w_ref in w_refs]
    outs = epi(accs, j)
    for o_ref, o in zip(out_refs, outs):
        o_ref[...] = o.astype(o_ref.dtype)


def _inproj(x2, mod3, g_pre, w, layer, part_offsets, n_col_tiles, epi, out_defs, *, tm, tn, seq_len,
            small_t=None, name="inproj"):
    n, d = x2.shape
    tiles_per_seq = seq_len // tm
    n_parts = len(part_offsets)
    in_specs = [pl.BlockSpec((tm, d), lambda i, j: (i, 0)),
                pl.BlockSpec((1, 1, d), lambda i, j: (i // tiles_per_seq, 0, 0)),
                pl.BlockSpec((1, 1, d), lambda i, j: (i // tiles_per_seq, 0, 1)),
                pl.BlockSpec((1, d), lambda i, j: (0, 0))]
    args = [x2, mod3, mod3, g_pre.reshape(1, d)]
    for off in part_offsets:
        in_specs.append(pl.BlockSpec((None, d, tn), lambda i, j, off=off: (layer, 0, off + j)))
        args.append(w)
    if small_t is not None:
        in_specs.append(pl.BlockSpec((GATE_ROWS, d), lambda i, j: (0, 0)))
        args.append(small_t)
    out_specs = [pl.BlockSpec((tm, tn), lambda i, j: (i, j)) for _ in out_defs]
    out_shape = [jax.ShapeDtypeStruct((n, n_col_tiles * tn), dt) for dt in out_defs]
    if small_t is not None:
        out_specs.append(pl.BlockSpec((GATE_ROWS, tm), lambda i, j: (0, i)))
        out_shape.append(jax.ShapeDtypeStruct((GATE_ROWS, n), F32))
    body = functools.partial(_inproj_body, n_parts=n_parts, n_out=len(out_defs), epi=epi,
                             with_small_t=small_t is not None, row_chunk=min(32, tm))
    return pl.pallas_call(
        body,
        grid=(n // tm, n_col_tiles),
        in_specs=in_specs,
        out_specs=out_specs,
        out_shape=out_shape,
        scratch_shapes=[pltpu.VMEM((tm, d), BF16)],
        compiler_params=_params("arbitrary", "arbitrary"),
        name=name,
    )(*args)


def _outproj_body(a_ref, w_ref, x_ref, gate_ref, gpost_ref, o_ref, *, row_chunk):
    o_ref[...] = _dot(a_ref[...], w_ref[...])
    tm = o_ref.shape[0]

    gain = gpost_ref[...] * gate_ref[0]

    for r in range(tm // row_chunk):
        rows = slice(r * row_chunk, (r + 1) * row_chunk)
        y = o_ref[rows, :]
        o_ref[rows, :] = x_ref[rows, :] + y * lax.rsqrt(jnp.mean(y * y, axis=-1, keepdims=True) + RMS_EPS) * gain


def _outproj(a, w, x2, mod3, g_post, *, tm, seq_len, name="outproj"):
    n, kdim = a.shape
    d = w.shape[1]
    tiles_per_seq = seq_len // tm
    body = functools.partial(_outproj_body, row_chunk=min(32, tm))
    return pl.pallas_call(
        body,
        grid=(n // tm,),
        in_specs=[pl.BlockSpec((tm, kdim), lambda i: (i, 0)),
                  pl.BlockSpec((kdim, d), lambda i: (0, 0), pipeline_mode=pl.Buffered(1)),
                  pl.BlockSpec((tm, d), lambda i: (i, 0)),
                  pl.BlockSpec((1, 1, d), lambda i: (i // tiles_per_seq, 0, 2)),
                  pl.BlockSpec((1, d), lambda i: (0, 0))],
        out_specs=pl.BlockSpec((tm, d), lambda i: (i, 0)),
        out_shape=jax.ShapeDtypeStruct((n, d), F32),
        compiler_params=_params("arbitrary"),
        name=name,
    )(a, w, x2, mod3, g_post.reshape(1, d))


def _dwconv_body(*refs, width, halo, col_chunk, row_tile, mode):
    if mode == "conformer":
        cur_ref, halo_ref, w_ref, b_ref, lng_ref, lnb_ref, o_ref, win_ref, sh_ref, y_ref = refs
    else:
        cur_ref, halo_ref, w_ref, gatein_ref, o_ref, win_ref, sh_ref, y_ref = refs
    t = pl.program_id(1)
    tt, d = y_ref.shape
    off = halo - (width - 1)
    sh_rows = sh_ref.shape[1]

    win_ref[0:halo, :] = jnp.where(t == 0, 0.0, halo_ref[0])
    win_ref[halo:halo + tt, :] = cur_ref[0]

    def col_body(c, carry):
        cs = pl.ds(pl.multiple_of(c * col_chunk, col_chunk), col_chunk)
        sh_ref[0] = win_ref[:, cs]
        for s in range(1, 8):
            sh_ref[s, 0:sh_rows - 8, :] = win_ref[pl.ds(s, sh_rows - 8), cs]
        for r in range(tt // row_tile):
            acc = jnp.zeros((row_tile, col_chunk), F32)
            for k in range(width):
                phase = (off + k) % 8
                base = off + k - phase
                acc = acc + w_ref[k:k + 1, cs] * sh_ref[phase, pl.ds(r * row_tile + base, row_tile), :]
            y_ref[pl.ds(r * row_tile, row_tile), cs] = acc
        return carry
    lax.fori_loop(0, d // col_chunk, col_body, 0)

    def tail(r, carry):
        rows = pl.ds(pl.multiple_of(r * 16, 16), 16)
        if mode == "conformer":
            u = y_ref[rows, :] + b_ref[...]
            mu = jnp.mean(u, axis=-1, keepdims=True)
            var = jnp.mean(jnp.square(u - mu), axis=-1, keepdims=True)
            un = (u - mu) * lax.rsqrt(var + LN_EPS) * lng_ref[...] + lnb_ref[...]
            o_ref[0, rows, :] = _silu(un).astype(o_ref.dtype)
        else:
            o_ref[0, rows, :] = (gatein_ref[0, rows, :] * y_ref[rows, :]).astype(o_ref.dtype)
        return carry
    lax.fori_loop(0, tt // 16, tail, 0, unroll=4)


def _dwconv(u3, w, *, mode, extra, tt):
    bsz, seq, d = u3.shape
    width = w.shape[0]
    halo = 8 * (-(-(width - 1) // 8))
    w_pad = jnp.pad(w, ((0, halo - width), (0, 0)))
    tiles = tt // halo
    in_specs = [pl.BlockSpec((1, tt, d), lambda b, t: (b, t, 0)),
                pl.BlockSpec((1, halo, d), lambda b, t: (b, jnp.maximum(t * tiles - 1, 0), 0)),
                pl.BlockSpec((halo, d), lambda b, t: (0, 0))]
    args = [u3, u3, w_pad]
    if mode == "conformer":
        for v in extra:
            in_specs.append(pl.BlockSpec((1, d), lambda b, t: (0, 0)))
            args.append(v.reshape(1, d))
    else:
        in_specs.append(pl.BlockSpec((1, tt, d), lambda b, t: (b, t, 0)))
        args.append(extra[0])
    col_chunk = min(256, d)
    body = functools.partial(_dwconv_body, width=width, halo=halo, col_chunk=col_chunk,
                             row_tile=min(64, tt), mode=mode)
    return pl.pallas_call(
        body,
        grid=(bsz, seq // tt),
        in_specs=in_specs,
        out_specs=pl.BlockSpec((1, tt, d), lambda b, t: (b, t, 0)),
        out_shape=jax.ShapeDtypeStruct((bsz, seq, d), BF16),
        scratch_shapes=[pltpu.VMEM((halo + tt, d), F32),
                        pltpu.VMEM((8, halo + tt, col_chunk), F32),
                        pltpu.VMEM((tt, d), F32)],
        compiler_params=_params("arbitrary", "arbitrary"),
        name="dwconv_" + mode,
    )(*args)


def _split3_f32(x):
    hi = x.astype(BF16).astype(F32)
    r1 = x - hi
    mid = r1.astype(BF16).astype(F32)
    lo = (r1 - mid).astype(BF16).astype(F32)
    return hi, mid, lo


def _mlstm_body(q_ref, k_ref, v_ref, o_ref, gt_ref, bcol_ref, ng_ref, out_ref, c_ref, m_ref, *, heads):
    c = pl.program_id(1)
    L = q_ref.shape[1]
    dk = q_ref.shape[2] // heads
    dv = v_ref.shape[2] // heads
    rep = dv // L

    @pl.when(c == 0)
    def _():
        c_ref[...] = jnp.zeros_like(c_ref)
        m_ref[...] = jnp.zeros_like(m_ref)

    g = gt_ref[...] + bcol_ref[...]
    g = GATE_SOFTCAP * jnp.tanh(g / GATE_SOFTCAP)
    cum = _cumsum_lanes(_log_sigmoid(g))
    li = g[0:heads]
    b = cum[heads:2 * heads]
    r = li - b
    lane = lax.broadcasted_iota(jnp.int32, (heads, L), 1)
    pm = r
    shift = 1
    while shift < L:
        pm = jnp.maximum(pm, jnp.where(lane >= shift, pltpu.roll(pm, shift, 1), -jnp.inf))
        shift *= 2
    m_prev = m_ref[...]
    u = jnp.maximum(m_prev, pm)
    u_last = jnp.broadcast_to(u[:, L - 1:L], (heads, L))
    b_last = jnp.broadcast_to(b[:, L - 1:L], (heads, L))
    m_ref[...] = b_last + u_last
    decay = jnp.exp(m_prev - u_last)
    lhs_terms = _split3_f32(-u) + _split3_f32(m_prev - u) + _split3_f32(-(b + u)) + _split3_f32(r - u_last)
    lhs_rows = (0, 1, 2, 6, 7, 8, 9, 10, 11, 12, 13, 14)
    r_terms = _split3_f32(r)

    rid = lax.broadcasted_iota(jnp.int32, (GATE_ROWS, L), 0)
    lhs_const = jnp.where((rid >= 3) & (rid < 6), 1.0, 0.0)
    rhs_blocks = [jnp.where((rid >= lo) & (rid < lo + 3), 1.0, 0.0).astype(BF16) for lo in (6, 9, 12)]
    rhs0_const = jnp.where(rid < 3, 1.0, 0.0)

    row = lax.broadcasted_iota(jnp.int32, (L, L), 0)
    col = lax.broadcasted_iota(jnp.int32, (L, L), 1)
    causal = col <= row
    ones_ll = jnp.ones((L, L), BF16)
    ones_dv = jnp.ones((dv, L), BF16)

    H = range(heads)
    qs = [q_ref[0, :, h * dk:(h + 1) * dk] for h in H]
    ks = [k_ref[0, :, h * dk:(h + 1) * dk] for h in H]
    vs = [v_ref[0, :, h * dv:(h + 1) * dv] for h in H]
    caugs = [c_ref[h] for h in H]

    def operands(h):
        lhs = lhs_const
        for i, term in zip(lhs_rows, lhs_terms):
            lhs = jnp.where(rid == i, term[h:h + 1, :], lhs)
        rhs0 = rhs0_const
        for i, term in zip((3, 4, 5), r_terms):
            rhs0 = jnp.where(rid == i, term[h:h + 1, :], rhs0)
        return lhs.astype(BF16), jnp.concatenate([rhs0.astype(BF16)] + rhs_blocks, axis=1)

    ops = [operands(h) for h in H]
    zs = [_dot_tn(lhs, rhs) for lhs, rhs in ops]
    qks = [_dot_nt(qs[h], ks[h]) for h in H]
    qcs = [_dot(qs[h], caugs[h].astype(BF16)) for h in H]
    ps = [jnp.exp(jnp.where(causal, z[:, 0:L], -jnp.inf)) for z in zs]
    ss = [(qks[h] * ps[h]).astype(BF16) for h in H]
    svs = [_dot(ss[h], jnp.concatenate([vs[h], ones_ll], axis=1)) for h in H]
    nds = [svs[h] + jnp.concatenate([jnp.exp(zs[h][:, L:2 * L])] * (rep + 1), axis=1) * qcs[h] for h in H]
    scales = [1.0 / jnp.maximum(jnp.abs(nds[h][:, dv:]), jnp.exp(zs[h][:, 2 * L:3 * L])) for h in H]
    houts = [nds[h][:, :dv] * jnp.concatenate([scales[h]] * rep, axis=1) for h in H]

    for h in H:
        wk = jnp.exp(zs[h][:, 3 * L:4 * L])
        vw = jnp.concatenate([vs[h].astype(F32) * jnp.concatenate([wk] * rep, axis=1), wk], axis=1)
        dec = jnp.concatenate([jnp.broadcast_to(decay[h:h + 1, :], (dk, L))] * (rep + 1), axis=1)
        c_ref[h] = dec * caugs[h] + _dot_tn(ks[h], vw.astype(BF16))

    def mean_sq(x):
        xx = x * x
        hi = xx.astype(BF16)
        lo = (xx - hi.astype(F32)).astype(BF16)
        return (_dot(hi, ones_dv) + _dot(lo, ones_dv)) * (1.0 / dv)

    mss = [mean_sq(houts[h]) for h in H]
    for h in H:
        hn = houts[h] * jnp.concatenate([lax.rsqrt(mss[h] + RMS_EPS)] * rep, axis=1)
        hn = hn * ng_ref[:, h * dv:(h + 1) * dv]
        og = _sigmoid(o_ref[0, :, h * dv:(h + 1) * dv].astype(F32))
        out_ref[0, :, h * dv:(h + 1) * dv] = (og * hn).astype(out_ref.dtype)


def _mlstm_rec(qkvo3, gt, b_if, norm_g, *, heads, chunk):
    bsz, seq, _ = qkvo3.shape
    dvt = norm_g.shape[0]
    dkt = dvt // 2
    nc = seq // chunk
    assert (dvt // heads) % chunk == 0 and 2 * heads <= GATE_ROWS
    bcol = jnp.pad(b_if, (0, GATE_ROWS - b_if.shape[0])).reshape(GATE_ROWS, 1)
    body = functools.partial(_mlstm_body, heads=heads)
    return pl.pallas_call(
        body,
        grid=(bsz, nc),
        in_specs=[pl.BlockSpec((1, chunk, dkt), lambda b, c: (b, c, 0)),
                  pl.BlockSpec((1, chunk, dkt), lambda b, c: (b, c, 1)),
                  pl.BlockSpec((1, chunk, dvt), lambda b, c: (b, c, 1)),
                  pl.BlockSpec((1, chunk, dvt), lambda b, c: (b, c, 2)),
                  pl.BlockSpec((GATE_ROWS, chunk), lambda b, c: (0, b * nc + c)),
                  pl.BlockSpec((GATE_ROWS, 1), lambda b, c: (0, 0)),
                  pl.BlockSpec((1, dvt), lambda b, c: (0, 0))],
        out_specs=pl.BlockSpec((1, chunk, dvt), lambda b, c: (b, c, 0)),
        out_shape=jax.ShapeDtypeStruct((bsz, seq, dvt), BF16),
        scratch_shapes=[pltpu.VMEM((heads, dkt // heads, dvt // heads + chunk), F32),
                        pltpu.VMEM((heads, chunk), F32)],
        compiler_params=_params("arbitrary", "arbitrary"),
        name="mlstm_rec",
    )(qkvo3, qkvo3, qkvo3, qkvo3, gt, bcol, norm_g.reshape(1, dvt))


def _gla_level_masks(chunk):
    t = np.arange(chunk)[:, None]
    s = np.arange(chunk)[None, :]
    masks = []
    b = chunk // 2
    while b >= 1:
        masks.append((t // (2 * b) == s // (2 * b)) & ((t % (2 * b)) >= b) & ((s % (2 * b)) < b))
        b //= 2
    return jnp.asarray(np.stack(masks), dtype=F32)


def _mid_rows(bc, b, row):
    L, dk = bc.shape
    if b >= 4:
        return jnp.concatenate([jnp.broadcast_to(bc[m:m + 1, :], (2 * b, dk)) for m in range(b, L, 2 * b)],
                               axis=0)
    up1 = pltpu.roll(bc, L - 1, 0)
    if b == 1:
        return jnp.where((row & 1) == 0, up1, bc)
    up2 = pltpu.roll(bc, L - 2, 0)
    down1 = pltpu.roll(bc, 1, 0)
    r4 = row & 3
    return jnp.where(r4 == 0, up2, jnp.where(r4 == 1, up1, jnp.where(r4 == 2, bc, down1)))


def _gla_body(q_ref, k_ref, v_ref, r_ref, al_ref, wa_ref, ba_ref, br_ref, ng_ref, mask_ref,
              out_ref, st_ref, *, heads):
    c = pl.program_id(1)
    L = q_ref.shape[1]
    dk = q_ref.shape[2] // heads
    dv = v_ref.shape[2] // heads
    n_levels = mask_ref.shape[0]

    @pl.when(c == 0)
    def _():
        st_ref[...] = jnp.zeros_like(st_ref)

    z2 = _dot_tn(al_ref[...].astype(BF16), wa_ref[...])
    z = jnp.where(lax.rem(c, 2) == 0, z2[0:L, :], z2[L:2 * L, :]) + ba_ref[...]
    log_alpha = _log_sigmoid(z) / GLA_TEMP
    bc_all = _cumsum_rows(log_alpha)

    eye = (lax.broadcasted_iota(jnp.int32, (L, L), 0) == lax.broadcasted_iota(jnp.int32, (L, L), 1))
    row = lax.broadcasted_iota(jnp.int32, (L, dk), 0)

    H = range(heads)
    qfs = [q_ref[0, :, h * dk:(h + 1) * dk].astype(F32) for h in H]
    kfs = [k_ref[0, :, h * dk:(h + 1) * dk].astype(F32) for h in H]
    vs = [v_ref[0, :, h * dv:(h + 1) * dv] for h in H]
    bcs = [bc_all[:, h * dk:(h + 1) * dk] for h in H]
    sts = [st_ref[h] for h in H]

    os_ = [_dot_nt((qfs[h] * jnp.exp(bcs[h])).astype(BF16), sts[h].astype(BF16)) for h in H]

    accs = [jnp.where(eye, jnp.sum(qfs[h] * kfs[h], axis=-1, keepdims=True), 0.0) for h in H]
    for i in range(n_levels):
        b = L >> (i + 1)
        upper = (row & (2 * b - 1)) >= b
        cls = [(jnp.where(upper, qfs[h], kfs[h])
                * jnp.exp(-jnp.abs(bcs[h] - _mid_rows(bcs[h], b, row)))).astype(BF16) for h in H]
        accs = [accs[h] + mask_ref[i] * _dot_nt(cls[h], cls[h]) for h in H]
    os_ = [os_[h] + _dot(accs[h].astype(BF16), vs[h]) for h in H]

    for h in H:
        b_last = bcs[h][L - 1:L, :]
        k_up = (kfs[h] * jnp.exp(jnp.minimum(b_last - bcs[h], 0.0))).astype(BF16)
        st_ref[h] = sts[h] * jnp.exp(b_last) + _dot_tn(vs[h], k_up)

    for h in H:
        o = os_[h]
        on = o * lax.rsqrt(jnp.mean(o * o, axis=-1, keepdims=True) + RMS_EPS)
        on = on * ng_ref[:, h * dv:(h + 1) * dv]
        rg = r_ref[0, :, h * dv:(h + 1) * dv].astype(F32) + br_ref[:, h * dv:(h + 1) * dv]
        out_ref[0, :, h * dv:(h + 1) * dv] = (on * _silu(rg)).astype(out_ref.dtype)


def _gla_rec(qkvr3, al_t, w_alpha, b_alpha, b_r, norm_g, *, heads, chunk):
    bsz, seq, _ = qkvr3.shape
    dvt = norm_g.shape[0]
    dkt = dvt // 2
    nc = seq // chunk
    rank = w_alpha.shape[0]
    assert rank <= GATE_ROWS and nc % 2 == 0
    wa = jnp.pad(w_alpha, ((0, GATE_ROWS - rank), (0, 0))).astype(BF16)
    masks = _gla_level_masks(chunk)
    body = functools.partial(_gla_body, heads=heads)
    return pl.pallas_call(
        body,
        grid=(bsz, seq // chunk),
        in_specs=[pl.BlockSpec((1, chunk, dkt), lambda b, c: (b, c, 0)),
                  pl.BlockSpec((1, chunk, dkt), lambda b, c: (b, c, 1)),
                  pl.BlockSpec((1, chunk, dvt), lambda b, c: (b, c, 1)),
                  pl.BlockSpec((1, chunk, dvt), lambda b, c: (b, c, 2)),
                  pl.BlockSpec((GATE_ROWS, 2 * chunk), lambda b, c: (0, (b * nc + c) // 2)),
                  pl.BlockSpec((GATE_ROWS, dkt), lambda b, c: (0, 0)),
                  pl.BlockSpec((1, dkt), lambda b, c: (0, 0)),
                  pl.BlockSpec((1, dvt), lambda b, c: (0, 0)),
                  pl.BlockSpec((1, dvt), lambda b, c: (0, 0)),
                  pl.BlockSpec(masks.shape, lambda b, c: (0, 0, 0))],
        out_specs=pl.BlockSpec((1, chunk, dvt), lambda b, c: (b, c, 0)),
        out_shape=jax.ShapeDtypeStruct((bsz, seq, dvt), BF16),
        scratch_shapes=[pltpu.VMEM((heads, dvt // heads, dkt // heads), F32)],
        compiler_params=_params("arbitrary", "arbitrary"),
        name="gla_rec",
    )(qkvr3, qkvr3, qkvr3, qkvr3, al_t, wa, b_alpha.reshape(1, dkt), b_r.reshape(1, dvt),
      norm_g.reshape(1, dvt), masks)


def _epi_glu(accs, j):
    a, g = accs
    return [a * _sigmoid(g)]


def _epi_swiglu(accs, j):
    g, u = accs
    return [_silu(g) * u]


def _epi_sconv(accs, j):
    bg, cg, xv = accs
    return [bg, cg * xv]


def _epi_scale_first(accs, j, *, n_first, scale):
    (a,) = accs
    return [a * jnp.where(j < n_first, scale, 1.0)]


def _pad_cols(w, cols):
    return jnp.pad(w, ((0, 0), (0, cols - w.shape[1])))


def kernel(x, c, conf_w_in, conf_w_dw, conf_b_dw, conf_ln_g, conf_ln_b, conf_w_out, sconv_w_in, sconv_w_conv, sconv_w_out, mlstm_w_in, mlstm_b_if, mlstm_norm_g, mlstm_w_out, gla_w_in, gla_w_alpha, gla_b_alpha, gla_b_r, gla_norm_g, gla_w_out, ada_w, ada_b, norm_pre, norm_post, ffn_w_gate_up, ffn_w_down):
    bsz, seq, d = x.shape
    n = bsz * seq
    depth = ada_w.shape[0]
    d_ff = ffn_w_down.shape[1]
    tm_in = _tile(1024, seq)
    tm_out = _tile(512, seq)
    tt = _tile(256, seq)

    mods = _ada_mods(c, ada_w, ada_b)
    x2 = x.reshape(n, d)

    for i in range(depth):
        kind, jj = i % 4, i // 4
        mod3 = mods[2 * i].reshape(bsz, 1, 3 * d)
        g_pre, g_post = norm_pre[i, 0], norm_post[i, 0]
        if kind == 0:
            tn = _tile(512, d)
            (u,) = _inproj(x2, mod3, g_pre, conf_w_in, jj, [0, d // tn], d // tn, _epi_glu,
                           [F32], tm=tm_in, tn=tn, seq_len=seq, name="conf_in")
            a = _dwconv(u.reshape(bsz, seq, d), conf_w_dw[jj], mode="conformer",
                        extra=(conf_b_dw[jj], conf_ln_g[jj], conf_ln_b[jj]), tt=tt)
            w_out = conf_w_out[jj]
        elif kind == 1:
            tn = _tile(512, d)
            bg, p = _inproj(x2, mod3, g_pre, sconv_w_in.astype(BF16), jj, [0, d // tn, 2 * d // tn],
                            d // tn, _epi_sconv, [F32, F32], tm=tm_in, tn=tn, seq_len=seq, name="sconv_in")
            a = _dwconv(p.reshape(bsz, seq, d), sconv_w_conv[jj], mode="sconv",
                        extra=(bg.reshape(bsz, seq, d),), tt=tt)
            w_out = sconv_w_out[jj]
        elif kind == 2:
            heads = MLSTM_HEADS
            qk, vd = d // 2, d
            w_in = mlstm_w_in[jj]
            w_if = w_in[:, 2 * qk + 2 * vd:]
            tn = _tile(1024, qk)
            epi = functools.partial(_epi_scale_first, n_first=qk // tn, scale=float((qk // heads) ** -0.5))
            qkvo, gt = _inproj(x2, mod3, g_pre, mlstm_w_in, jj, [0], (2 * qk + 2 * vd) // tn, epi,
                               [BF16], tm=tm_in, tn=tn, seq_len=seq,
                               small_t=_pad_cols(w_if, GATE_ROWS).T.astype(BF16), name="mlstm_in")
            a = _mlstm_rec(qkvo.reshape(bsz, seq, -1), gt, mlstm_b_if[jj], mlstm_norm_g[jj], heads=heads,
                           chunk=MLSTM_CHUNK)
            w_out = mlstm_w_out[jj]
        else:
            heads = GLA_HEADS
            kd, vd = d // 2, d
            w_in = gla_w_in[jj]
            w_al = w_in[:, 2 * kd + 2 * vd:]
            tn = _tile(1024, kd)
            epi = functools.partial(_epi_scale_first, n_first=kd // tn, scale=float((kd // heads) ** -0.5))
            qkvr, al_t = _inproj(x2, mod3, g_pre, gla_w_in, jj, [0], (2 * kd + 2 * vd) // tn, epi,
                                 [BF16], tm=tm_in, tn=tn, seq_len=seq,
                                 small_t=_pad_cols(w_al, GATE_ROWS).T.astype(BF16), name="gla_in")
            a = _gla_rec(qkvr.reshape(bsz, seq, -1), al_t, gla_w_alpha[jj],
                         gla_b_alpha[jj], gla_b_r[jj], gla_norm_g[jj], heads=heads, chunk=GLA_CHUNK)
            w_out = gla_w_out[jj]
        x2 = _outproj(a.reshape(n, d), w_out.astype(BF16), x2, mod3, g_post, tm=tm_out, seq_len=seq,
                      name="mixer_out")

        mod3 = mods[2 * i + 1].reshape(bsz, 1, 3 * d)
        tn = _tile(512, d_ff)
        (act,) = _inproj(x2, mod3, norm_pre[i, 1], ffn_w_gate_up, i, [0, d_ff // tn],
                         d_ff // tn, _epi_swiglu, [BF16], tm=tm_in, tn=tn, seq_len=seq, name="ffn_in")
        x2 = _outproj(act, ffn_w_down[i].astype(BF16), x2, mod3, norm_post[i, 1], tm=tm_out, seq_len=seq,
                      name="ffn_out")

    return x2.reshape(bsz, seq, d)
```

```python
import functools

import numpy as np
import jax
import jax.numpy as jnp
from jax import lax
from jax.experimental import pallas as pl
from jax.experimental.pallas import tpu as pltpu

F32 = jnp.float32
BF16 = jnp.bfloat16

RMS_EPS = 1e-6
LN_EPS = 1e-5
MLSTM_HEADS = 8
MLSTM_CHUNK = 128
GATE_SOFTCAP = 15.0
GLA_HEADS = 4
GLA_CHUNK = 64
GLA_TEMP = 16.0

LANE = 128
GATE_ROWS = 16
VMEM_LIMIT_BYTES = 56 * 1024 * 1024


def _tile(pref, dim):
    return pref if dim % pref == 0 else dim


def _params(*sem):
    return pltpu.CompilerParams(dimension_semantics=sem, vmem_limit_bytes=VMEM_LIMIT_BYTES)


def _sigmoid(z):
    return 1.0 / (1.0 + jnp.exp(-z))


def _silu(z):
    return z * _sigmoid(z)


def _log_sigmoid(z):
    return jnp.minimum(z, 0.0) - jnp.log1p(jnp.exp(-jnp.abs(z)))


def _dot(a, b):
    return jnp.dot(a, b, preferred_element_type=F32)


def _dot_nt(a, b):
    return lax.dot_general(a, b, (((1,), (1,)), ((), ())), preferred_element_type=F32)


def _dot_tn(a, b):
    return lax.dot_general(a, b, (((0,), (0,)), ((), ())), preferred_element_type=F32)


def _split3(x):
    hi = x.astype(BF16)
    r1 = x - hi.astype(F32)
    mid = r1.astype(BF16)
    lo = (r1 - mid.astype(F32)).astype(BF16)
    return hi, mid, lo


def _tri(n, upper):
    r = lax.broadcasted_iota(jnp.int32, (n, n), 0)
    c = lax.broadcasted_iota(jnp.int32, (n, n), 1)
    return jnp.where((r <= c) if upper else (c <= r), 1.0, 0.0).astype(BF16)


def _cumsum_rows(x):
    tri = _tri(x.shape[0], upper=False)
    hi, mid, lo = _split3(x)
    return _dot(tri, hi) + _dot(tri, mid) + _dot(tri, lo)


def _cumsum_lanes(x):
    tri = _tri(x.shape[1], upper=True)
    hi, mid, lo = _split3(x)
    return _dot(hi, tri) + _dot(mid, tri) + _dot(lo, tri)


def _ada_body(c_ref, w_ref, b_ref, o_ref):
    sc = _silu(c_ref[...]).astype(BF16)
    o_ref[0] = _dot(sc, w_ref[0].astype(BF16)) + b_ref[0]


def _ada_mods(c, ada_w, ada_b):
    depth, two, d, d3 = ada_w.shape
    n_sub = depth * two
    bsz = c.shape[0]
    rows = -(-bsz // 16) * 16
    c_pad = jnp.pad(c, ((0, rows - bsz), (0, 0)))
    tn = _tile(1024, d3)
    out = pl.pallas_call(
        _ada_body,
        grid=(n_sub, d3 // tn),
        in_specs=[pl.BlockSpec((rows, d), lambda l, j: (0, 0)),
                  pl.BlockSpec((1, d, tn), lambda l, j: (l, 0, j)),
                  pl.BlockSpec((1, 1, tn), lambda l, j: (l, 0, j))],
        out_specs=pl.BlockSpec((1, rows, tn), lambda l, j: (l, 0, j)),
        out_shape=jax.ShapeDtypeStruct((n_sub, rows, d3), F32),
        compiler_params=_params("arbitrary", "arbitrary"),
        name="ada_mods",
    )(c_pad, ada_w.reshape(n_sub, d, d3), ada_b.reshape(n_sub, 1, d3))
    return out[:, :bsz, :]


def _inproj_body(*refs, n_parts, n_out, epi, with_small_t, row_chunk):
    x_ref, shift_ref, scale_ref, gpre_ref = refs[:4]
    w_refs = refs[4:4 + n_parts]
    pos = 4 + n_parts
    wst_ref = gt_ref = None
    if with_small_t:
        wst_ref = refs[pos]
        pos += 1
    out_refs = refs[pos:pos + n_out]
    pos += n_out
    if with_small_t:
        gt_ref = refs[pos]
        pos += 1
    hn_ref = refs[pos]

    j = pl.program_id(1)
    tm = x_ref.shape[0]

    @pl.when(j == 0)
    def _():
        gain = gpre_ref[...] * (1.0 + scale_ref[0])
        shift = shift_ref[0]

        def chunk(r, carry):
            rows = pl.ds(pl.multiple_of(r * row_chunk, row_chunk), row_chunk)
            x = x_ref[rows, :]
            h = x * lax.rsqrt(jnp.mean(x * x, axis=-1, keepdims=True) + RMS_EPS) * gain + shift
            hn_ref[rows, :] = h.astype(BF16)
            return carry
        lax.fori_loop(0, tm // row_chunk, chunk, 0, unroll=4)
        if with_small_t:
            gt_ref[...] = _dot_nt(wst_ref[...].astype(BF16), hn_ref[...])

    hn = hn_ref[...]
    accs = [_dot(hn, w_ref[...].astype(BF16)) for w_ref in w_refs]
    outs = epi(accs, j)
    for o_ref, o in zip(out_refs, outs):
        o_ref[...] = o.astype(o_ref.dtype)


def _inproj(x2, mod3, g_pre, w, layer, part_offsets, n_col_tiles, epi, out_defs, *, tm, tn, seq_len,
            small_t=None, name="inproj"):
    n, d = x2.shape
    tiles_per_seq = seq_len // tm
    n_parts = len(part_offsets)
    in_specs = [pl.BlockSpec((tm, d), lambda i, j: (i, 0)),
                pl.BlockSpec((1, 1, d), lambda i, j: (i // tiles_per_seq, 0, 0)),
                pl.BlockSpec((1, 1, d), lambda i, j: (i // tiles_per_seq, 0, 1)),
                pl.BlockSpec((1, d), lambda i, j: (0, 0))]
    args = [x2, mod3, mod3, g_pre.reshape(1, d)]
    for off in part_offsets:
        in_specs.append(pl.BlockSpec((None, d, tn), lambda i, j, off=off: (layer, 0, off + j)))
        args.append(w)
    if small_t is not None:
        in_specs.append(pl.BlockSpec((GATE_ROWS, d), lambda i, j: (0, 0)))
        args.append(small_t)
    out_specs = [pl.BlockSpec((tm, tn), lambda i, j: (i, j)) for _ in out_defs]
    out_shape = [jax.ShapeDtypeStruct((n, n_col_tiles * tn), dt) for dt in out_defs]
    if small_t is not None:
        out_specs.append(pl.BlockSpec((GATE_ROWS, tm), lambda i, j: (0, i)))
        out_shape.append(jax.ShapeDtypeStruct((GATE_ROWS, n), F32))
    body = functools.partial(_inproj_body, n_parts=n_parts, n_out=len(out_defs), epi=epi,
                             with_small_t=small_t is not None, row_chunk=min(32, tm))
    return pl.pallas_call(
        body,
        grid=(n // tm, n_col_tiles),
        in_specs=in_specs,
        out_specs=out_specs,
        out_shape=out_shape,
        scratch_shapes=[pltpu.VMEM((tm, d), BF16)],
        compiler_params=_params("arbitrary", "arbitrary"),
        name=name,
    )(*args)


def _outproj_body(*refs, row_chunk, with_next):
    if with_next:
        a_ref, w_ref, x_ref, gate_ref, gpost_ref, nshift_ref, nscale_ref, ngpre_ref, o_ref, hn_ref = refs
    else:
        a_ref, w_ref, x_ref, gate_ref, gpost_ref, o_ref = refs
    o_ref[...] = _dot(a_ref[...], w_ref[...])
    tm = o_ref.shape[0]

    gain = gpost_ref[...] * gate_ref[0]
    if with_next:
        next_gain = ngpre_ref[...] * (1.0 + nscale_ref[0])
        next_shift = nshift_ref[0]

    for r in range(tm // row_chunk):
        rows = slice(r * row_chunk, (r + 1) * row_chunk)
        y = o_ref[rows, :]
        xn = x_ref[rows, :] + y * lax.rsqrt(jnp.mean(y * y, axis=-1, keepdims=True) + RMS_EPS) * gain
        o_ref[rows, :] = xn
        if with_next:
            h = xn * lax.rsqrt(jnp.mean(xn * xn, axis=-1, keepdims=True) + RMS_EPS) * next_gain + next_shift
            hn_ref[rows, :] = h.astype(BF16)


def _outproj(a, w, layer, x2, mod3, g_post, *, tm, seq_len, next_pre=None, name="outproj"):
    n, kdim = a.shape
    d = w.shape[2]
    tiles_per_seq = seq_len // tm
    with_next = next_pre is not None
    body = functools.partial(_outproj_body, row_chunk=min(32, tm), with_next=with_next)
    in_specs = [pl.BlockSpec((tm, kdim), lambda i: (i, 0)),
                pl.BlockSpec((None, kdim, d), lambda i: (layer, 0, 0), pipeline_mode=pl.Buffered(1)),
                pl.BlockSpec((tm, d), lambda i: (i, 0)),
                pl.BlockSpec((1, 1, d), lambda i: (i // tiles_per_seq, 0, 2)),
                pl.BlockSpec((1, d), lambda i: (0, 0))]
    args = [a, w, x2, mod3, g_post.reshape(1, d)]
    out_specs = [pl.BlockSpec((tm, d), lambda i: (i, 0))]
    out_shape = [jax.ShapeDtypeStruct((n, d), F32)]
    if with_next:
        next_mod3, next_g_pre = next_pre
        in_specs += [pl.BlockSpec((1, 1, d), lambda i: (i // tiles_per_seq, 0, 0)),
                     pl.BlockSpec((1, 1, d), lambda i: (i // tiles_per_seq, 0, 1)),
                     pl.BlockSpec((1, d), lambda i: (0, 0))]
        args += [next_mod3, next_mod3, next_g_pre.reshape(1, d)]
        out_specs.append(pl.BlockSpec((tm, d), lambda i: (i, 0)))
        out_shape.append(jax.ShapeDtypeStruct((n, d), BF16))
    outs = pl.pallas_call(
        body,
        grid=(n // tm,),
        in_specs=in_specs,
        out_specs=out_specs,
        out_shape=out_shape,
        compiler_params=_params("arbitrary"),
        name=name,
    )(*args)
    return outs if with_next else outs[0]


def _ffn_in_body(hn_ref, wg_ref, wu_ref, o_ref):
    hn = hn_ref[...]
    gate = _dot(hn, wg_ref[...].astype(BF16))
    up = _dot(hn, wu_ref[...].astype(BF16))
    o_ref[...] = (_silu(gate) * up).astype(o_ref.dtype)


def _ffn_in(hn, w, layer, *, tm, tn):
    n, d = hn.shape
    d_ff = w.shape[2] // 2
    nj = d_ff // tn
    return pl.pallas_call(
        _ffn_in_body,
        grid=(nj, n // tm),
        in_specs=[pl.BlockSpec((tm, d), lambda j, i: (i, 0)),
                  pl.BlockSpec((None, d, tn), lambda j, i: (layer, 0, j)),
                  pl.BlockSpec((None, d, tn), lambda j, i: (layer, 0, nj + j))],
        out_specs=pl.BlockSpec((tm, tn), lambda j, i: (i, j)),
        out_shape=jax.ShapeDtypeStruct((n, d_ff), BF16),
        compiler_params=_params("arbitrary", "arbitrary"),
        name="ffn_in",
    )(hn, w, w)


def _dwconv_body(*refs, width, halo, col_chunk, row_tile, mode):
    if mode == "conformer":
        cur_ref, halo_ref, w_ref, b_ref, lng_ref, lnb_ref, o_ref, win_ref, sh_ref, y_ref = refs
    else:
        cur_ref, halo_ref, w_ref, gatein_ref, o_ref, win_ref, sh_ref, y_ref = refs
    t = pl.program_id(1)
    tt, d = y_ref.shape
    off = halo - (width - 1)
    sh_rows = sh_ref.shape[1]

    win_ref[0:halo, :] = jnp.where(t == 0, 0.0, halo_ref[0])
    win_ref[halo:halo + tt, :] = cur_ref[0]

    def col_body(c, carry):
        cs = pl.ds(pl.multiple_of(c * col_chunk, col_chunk), col_chunk)
        sh_ref[0] = win_ref[:, cs]
        for s in range(1, 8):
            sh_ref[s, 0:sh_rows - 8, :] = win_ref[pl.ds(s, sh_rows - 8), cs]
        for r in range(tt // row_tile):
            acc = jnp.zeros((row_tile, col_chunk), F32)
            for k in range(width):
                phase = (off + k) % 8
                base = off + k - phase
                acc = acc + w_ref[k:k + 1, cs] * sh_ref[phase, pl.ds(r * row_tile + base, row_tile), :]
            y_ref[pl.ds(r * row_tile, row_tile), cs] = acc
        return carry
    lax.fori_loop(0, d // col_chunk, col_body, 0)

    def tail(r, carry):
        rows = pl.ds(pl.multiple_of(r * 16, 16), 16)
        if mode == "conformer":
            u = y_ref[rows, :] + b_ref[...]
            mu = jnp.mean(u, axis=-1, keepdims=True)
            var = jnp.mean(jnp.square(u - mu), axis=-1, keepdims=True)
            un = (u - mu) * lax.rsqrt(var + LN_EPS) * lng_ref[...] + lnb_ref[...]
            o_ref[0, rows, :] = _silu(un).astype(o_ref.dtype)
        else:
            o_ref[0, rows, :] = (gatein_ref[0, rows, :] * y_ref[rows, :]).astype(o_ref.dtype)
        return carry
    lax.fori_loop(0, tt // 16, tail, 0, unroll=4)


def _dwconv(u3, w, *, mode, extra, tt):
    bsz, seq, d = u3.shape
    width = w.shape[0]
    halo = 8 * (-(-(width - 1) // 8))
    w_pad = jnp.pad(w, ((0, halo - width), (0, 0)))
    tiles = tt // halo
    in_specs = [pl.BlockSpec((1, tt, d), lambda b, t: (b, t, 0)),
                pl.BlockSpec((1, halo, d), lambda b, t: (b, jnp.maximum(t * tiles - 1, 0), 0)),
                pl.BlockSpec((halo, d), lambda b, t: (0, 0))]
    args = [u3, u3, w_pad]
    if mode == "conformer":
        for v in extra:
            in_specs.append(pl.BlockSpec((1, d), lambda b, t: (0, 0)))
            args.append(v.reshape(1, d))
    else:
        in_specs.append(pl.BlockSpec((1, tt, d), lambda b, t: (b, t, 0)))
        args.append(extra[0])
    col_chunk = min(256, d)
    body = functools.partial(_dwconv_body, width=width, halo=halo, col_chunk=col_chunk,
                             row_tile=min(64, tt), mode=mode)
    return pl.pallas_call(
        body,
        grid=(bsz, seq // tt),
        in_specs=in_specs,
        out_specs=pl.BlockSpec((1, tt, d), lambda b, t: (b, t, 0)),
        out_shape=jax.ShapeDtypeStruct((bsz, seq, d), BF16),
        scratch_shapes=[pltpu.VMEM((halo + tt, d), F32),
                        pltpu.VMEM((8, halo + tt, col_chunk), F32),
                        pltpu.VMEM((tt, d), F32)],
        compiler_params=_params("arbitrary", "arbitrary"),
        name="dwconv_" + mode,
    )(*args)


def _split3_f32(x):
    hi = x.astype(BF16).astype(F32)
    r1 = x - hi
    mid = r1.astype(BF16).astype(F32)
    lo = (r1 - mid).astype(BF16).astype(F32)
    return hi, mid, lo


def _mlstm_body(q_ref, k_ref, v_ref, o_ref, gt_ref, bcol_ref, ng_ref, out_ref, c_ref, m_ref, *, heads):
    c = pl.program_id(1)
    L = q_ref.shape[1]
    dk = q_ref.shape[2] // heads
    dv = v_ref.shape[2] // heads
    rep = dv // L

    @pl.when(c == 0)
    def _():
        c_ref[...] = jnp.zeros_like(c_ref)
        m_ref[...] = jnp.zeros_like(m_ref)

    g = gt_ref[...] + bcol_ref[...]
    g = GATE_SOFTCAP * jnp.tanh(g / GATE_SOFTCAP)
    cum = _cumsum_lanes(_log_sigmoid(g))
    li = g[0:heads]
    b = cum[heads:2 * heads]
    r = li - b
    lane = lax.broadcasted_iota(jnp.int32, (heads, L), 1)
    pm = r
    shift = 1
    while shift < L:
        pm = jnp.maximum(pm, jnp.where(lane >= shift, pltpu.roll(pm, shift, 1), -jnp.inf))
        shift *= 2
    m_prev = m_ref[...]
    u = jnp.maximum(m_prev, pm)
    u_last = jnp.broadcast_to(u[:, L - 1:L], (heads, L))
    b_last = jnp.broadcast_to(b[:, L - 1:L], (heads, L))
    m_ref[...] = b_last + u_last
    decay = jnp.exp(m_prev - u_last)
    lhs_terms = _split3_f32(-u) + _split3_f32(m_prev - u) + _split3_f32(-(b + u)) + _split3_f32(r - u_last)
    lhs_rows = (0, 1, 2, 6, 7, 8, 9, 10, 11, 12, 13, 14)
    r_terms = _split3_f32(r)

    rid = lax.broadcasted_iota(jnp.int32, (GATE_ROWS, L), 0)
    lhs_const = jnp.where((rid >= 3) & (rid < 6), 1.0, 0.0)
    rhs_blocks = [jnp.where((rid >= lo) & (rid < lo + 3), 1.0, 0.0).astype(BF16) for lo in (6, 9, 12)]
    rhs0_const = jnp.where(rid < 3, 1.0, 0.0)

    row = lax.broadcasted_iota(jnp.int32, (L, L), 0)
    col = lax.broadcasted_iota(jnp.int32, (L, L), 1)
    causal = col <= row
    ones_ll = jnp.ones((L, L), BF16)
    ones_dv = jnp.ones((dv, L), BF16)

    H = range(heads)
    qs = [q_ref[0, :, h * dk:(h + 1) * dk] for h in H]
    ks = [k_ref[0, :, h * dk:(h + 1) * dk] for h in H]
    vs = [v_ref[0, :, h * dv:(h + 1) * dv] for h in H]
    caugs = [c_ref[h] for h in H]

    def operands(h):
        lhs = lhs_const
        for i, term in zip(lhs_rows, lhs_terms):
            lhs = jnp.where(rid == i, term[h:h + 1, :], lhs)
        rhs0 = rhs0_const
        for i, term in zip((3, 4, 5), r_terms):
            rhs0 = jnp.where(rid == i, term[h:h + 1, :], rhs0)
        return lhs.astype(BF16), jnp.concatenate([rhs0.astype(BF16)] + rhs_blocks, axis=1)

    ops = [operands(h) for h in H]
    zs = [_dot_tn(lhs, rhs) for lhs, rhs in ops]
    qks = [_dot_nt(qs[h], ks[h]) for h in H]
    qcs = [_dot(qs[h], caugs[h].astype(BF16)) for h in H]
    ps = [jnp.exp(jnp.where(causal, z[:, 0:L], -jnp.inf)) for z in zs]
    ss = [(qks[h] * ps[h]).astype(BF16) for h in H]
    svs = [_dot(ss[h], jnp.concatenate([vs[h], ones_ll], axis=1)) for h in H]
    nds = [svs[h] + jnp.concatenate([jnp.exp(zs[h][:, L:2 * L])] * (rep + 1), axis=1) * qcs[h] for h in H]
    scales = [1.0 / jnp.maximum(jnp.abs(nds[h][:, dv:]), jnp.exp(zs[h][:, 2 * L:3 * L])) for h in H]
    houts = [nds[h][:, :dv] * jnp.concatenate([scales[h]] * rep, axis=1) for h in H]

    for h in H:
        wk = jnp.exp(zs[h][:, 3 * L:4 * L])
        vw = jnp.concatenate([vs[h].astype(F32) * jnp.concatenate([wk] * rep, axis=1), wk], axis=1)
        dec = jnp.concatenate([jnp.broadcast_to(decay[h:h + 1, :], (dk, L))] * (rep + 1), axis=1)
        c_ref[h] = dec * caugs[h] + _dot_tn(ks[h], vw.astype(BF16))

    def mean_sq(x):
        xx = x * x
        hi = xx.astype(BF16)
        lo = (xx - hi.astype(F32)).astype(BF16)
        return (_dot(hi, ones_dv) + _dot(lo, ones_dv)) * (1.0 / dv)

    mss = [mean_sq(houts[h]) for h in H]
    for h in H:
        hn = houts[h] * jnp.concatenate([lax.rsqrt(mss[h] + RMS_EPS)] * rep, axis=1)
        hn = hn * ng_ref[:, h * dv:(h + 1) * dv]
        og = _sigmoid(o_ref[0, :, h * dv:(h + 1) * dv].astype(F32))
        out_ref[0, :, h * dv:(h + 1) * dv] = (og * hn).astype(out_ref.dtype)


def _mlstm_rec(qkvo3, gt, b_if, norm_g, *, heads, chunk):
    bsz, seq, _ = qkvo3.shape
    dvt = norm_g.shape[0]
    dkt = dvt // 2
    nc = seq // chunk
    assert (dvt // heads) % chunk == 0 and 2 * heads <= GATE_ROWS
    bcol = jnp.pad(b_if, (0, GATE_ROWS - b_if.shape[0])).reshape(GATE_ROWS, 1)
    body = functools.partial(_mlstm_body, heads=heads)
    return pl.pallas_call(
        body,
        grid=(bsz, nc),
        in_specs=[pl.BlockSpec((1, chunk, dkt), lambda b, c: (b, c, 0)),
                  pl.BlockSpec((1, chunk, dkt), lambda b, c: (b, c, 1)),
                  pl.BlockSpec((1, chunk, dvt), lambda b, c: (b, c, 1)),
                  pl.BlockSpec((1, chunk, dvt), lambda b, c: (b, c, 2)),
                  pl.BlockSpec((GATE_ROWS, chunk), lambda b, c: (0, b * nc + c)),
                  pl.BlockSpec((GATE_ROWS, 1), lambda b, c: (0, 0)),
                  pl.BlockSpec((1, dvt), lambda b, c: (0, 0))],
        out_specs=pl.BlockSpec((1, chunk, dvt), lambda b, c: (b, c, 0)),
        out_shape=jax.ShapeDtypeStruct((bsz, seq, dvt), BF16),
        scratch_shapes=[pltpu.VMEM((heads, dkt // heads, dvt // heads + chunk), F32),
                        pltpu.VMEM((heads, chunk), F32)],
        compiler_params=_params("arbitrary", "arbitrary"),
        name="mlstm_rec",
    )(qkvo3, qkvo3, qkvo3, qkvo3, gt, bcol, norm_g.reshape(1, dvt))


def _gla_level_masks(chunk):
    t = np.arange(chunk)[:, None]
    s = np.arange(chunk)[None, :]
    masks = []
    b = chunk // 2
    while b >= 1:
        masks.append((t // (2 * b) == s // (2 * b)) & ((t % (2 * b)) >= b) & ((s % (2 * b)) < b))
        b //= 2
    return jnp.asarray(np.stack(masks), dtype=F32)


def _mid_rows(bc, b, row):
    L, dk = bc.shape
    if b >= 4:
        return jnp.concatenate([jnp.broadcast_to(bc[m:m + 1, :], (2 * b, dk)) for m in range(b, L, 2 * b)],
                               axis=0)
    up1 = pltpu.roll(bc, L - 1, 0)
    if b == 1:
        return jnp.where((row & 1) == 0, up1, bc)
    up2 = pltpu.roll(bc, L - 2, 0)
    down1 = pltpu.roll(bc, 1, 0)
    r4 = row & 3
    return jnp.where(r4 == 0, up2, jnp.where(r4 == 1, up1, jnp.where(r4 == 2, bc, down1)))


def _gla_body(q_ref, k_ref, v_ref, r_ref, al_ref, wa_ref, ba_ref, br_ref, ng_ref, mask_ref,
              out_ref, st_ref, *, heads):
    c = pl.program_id(1)
    L = q_ref.shape[1]
    dk = q_ref.shape[2] // heads
    dv = v_ref.shape[2] // heads
    n_levels = mask_ref.shape[0]

    @pl.when(c == 0)
    def _():
        st_ref[...] = jnp.zeros_like(st_ref)

    z2 = _dot_tn(al_ref[...].astype(BF16), wa_ref[...])
    z = jnp.where(lax.rem(c, 2) == 0, z2[0:L, :], z2[L:2 * L, :]) + ba_ref[...]
    log_alpha = _log_sigmoid(z) / GLA_TEMP
    bc_all = _cumsum_rows(log_alpha)

    eye = (lax.broadcasted_iota(jnp.int32, (L, L), 0) == lax.broadcasted_iota(jnp.int32, (L, L), 1))
    row = lax.broadcasted_iota(jnp.int32, (L, dk), 0)

    H = range(heads)
    qfs = [q_ref[0, :, h * dk:(h + 1) * dk].astype(F32) for h in H]
    kfs = [k_ref[0, :, h * dk:(h + 1) * dk].astype(F32) for h in H]
    vs = [v_ref[0, :, h * dv:(h + 1) * dv] for h in H]
    bcs = [bc_all[:, h * dk:(h + 1) * dk] for h in H]
    sts = [st_ref[h] for h in H]

    os_ = [_dot_nt((qfs[h] * jnp.exp(bcs[h])).astype(BF16), sts[h].astype(BF16)) for h in H]

    accs = [jnp.where(eye, jnp.sum(qfs[h] * kfs[h], axis=-1, keepdims=True), 0.0) for h in H]
    for i in range(n_levels):
        b = L >> (i + 1)
        upper = (row & (2 * b - 1)) >= b
        cls = [(jnp.where(upper, qfs[h], kfs[h])
                * jnp.exp(-jnp.abs(bcs[h] - _mid_rows(bcs[h], b, row)))).astype(BF16) for h in H]
        accs = [accs[h] + mask_ref[i] * _dot_nt(cls[h], cls[h]) for h in H]
    os_ = [os_[h] + _dot(accs[h].astype(BF16), vs[h]) for h in H]

    for h in H:
        b_last = bcs[h][L - 1:L, :]
        k_up = (kfs[h] * jnp.exp(jnp.minimum(b_last - bcs[h], 0.0))).astype(BF16)
        st_ref[h] = sts[h] * jnp.exp(b_last) + _dot_tn(vs[h], k_up)

    for h in H:
        o = os_[h]
        on = o * lax.rsqrt(jnp.mean(o * o, axis=-1, keepdims=True) + RMS_EPS)
        on = on * ng_ref[:, h * dv:(h + 1) * dv]
        rg = r_ref[0, :, h * dv:(h + 1) * dv].astype(F32) + br_ref[:, h * dv:(h + 1) * dv]
        out_ref[0, :, h * dv:(h + 1) * dv] = (on * _silu(rg)).astype(out_ref.dtype)


def _gla_rec(qkvr3, al_t, w_alpha, b_alpha, b_r, norm_g, *, heads, chunk):
    bsz, seq, _ = qkvr3.shape
    dvt = norm_g.shape[0]
    dkt = dvt // 2
    nc = seq // chunk
    rank = w_alpha.shape[0]
    assert rank <= GATE_ROWS and nc % 2 == 0
    wa = jnp.pad(w_alpha, ((0, GATE_ROWS - rank), (0, 0))).astype(BF16)
    masks = _gla_level_masks(chunk)
    body = functools.partial(_gla_body, heads=heads)
    return pl.pallas_call(
        body,
        grid=(bsz, seq // chunk),
        in_specs=[pl.BlockSpec((1, chunk, dkt), lambda b, c: (b, c, 0)),
                  pl.BlockSpec((1, chunk, dkt), lambda b, c: (b, c, 1)),
                  pl.BlockSpec((1, chunk, dvt), lambda b, c: (b, c, 1)),
                  pl.BlockSpec((1, chunk, dvt), lambda b, c: (b, c, 2)),
                  pl.BlockSpec((GATE_ROWS, 2 * chunk), lambda b, c: (0, (b * nc + c) // 2)),
                  pl.BlockSpec((GATE_ROWS, dkt), lambda b, c: (0, 0)),
                  pl.BlockSpec((1, dkt), lambda b, c: (0, 0)),
                  pl.BlockSpec((1, dvt), lambda b, c: (0, 0)),
                  pl.BlockSpec((1, dvt), lambda b, c: (0, 0)),
                  pl.BlockSpec(masks.shape, lambda b, c: (0, 0, 0))],
        out_specs=pl.BlockSpec((1, chunk, dvt), lambda b, c: (b, c, 0)),
        out_shape=jax.ShapeDtypeStruct((bsz, seq, dvt), BF16),
        scratch_shapes=[pltpu.VMEM((heads, dvt // heads, dkt // heads), F32)],
        compiler_params=_params("arbitrary", "arbitrary"),
        name="gla_rec",
    )(qkvr3, qkvr3, qkvr3, qkvr3, al_t, wa, b_alpha.reshape(1, dkt), b_r.reshape(1, dvt),
      norm_g.reshape(1, dvt), masks)


def _epi_glu(accs, j):
    a, g = accs
    return [a * _sigmoid(g)]


def _epi_sconv(accs, j):
    bg, cg, xv = accs
    return [bg, cg * xv]


def _epi_scale_first(accs, j, *, n_first, scale):
    (a,) = accs
    return [a * jnp.where(j < n_first, scale, 1.0)]


def _pad_cols(w, cols):
    return jnp.pad(w, ((0, 0), (0, cols - w.shape[1])))


def kernel(x, c, conf_w_in, conf_w_dw, conf_b_dw, conf_ln_g, conf_ln_b, conf_w_out, sconv_w_in, sconv_w_conv, sconv_w_out, mlstm_w_in, mlstm_b_if, mlstm_norm_g, mlstm_w_out, gla_w_in, gla_w_alpha, gla_b_alpha, gla_b_r, gla_norm_g, gla_w_out, ada_w, ada_b, norm_pre, norm_post, ffn_w_gate_up, ffn_w_down):
    bsz, seq, d = x.shape
    n = bsz * seq
    depth = ada_w.shape[0]
    d_ff = ffn_w_down.shape[1]
    tm_in = _tile(1024, seq)
    tm_out = _tile(512, seq)
    tt = _tile(256, seq)

    mods = _ada_mods(c, ada_w, ada_b)
    x2 = x.reshape(n, d)
    w_down = ffn_w_down.astype(BF16)

    for i in range(depth):
        kind, jj = i % 4, i // 4
        mod3 = mods[2 * i].reshape(bsz, 1, 3 * d)
        g_pre, g_post = norm_pre[i, 0], norm_post[i, 0]
        if kind == 0:
            tn = _tile(512, d)
            (u,) = _inproj(x2, mod3, g_pre, conf_w_in, jj, [0, d // tn], d // tn, _epi_glu,
                           [F32], tm=tm_in, tn=tn, seq_len=seq, name="conf_in")
            a = _dwconv(u.reshape(bsz, seq, d), conf_w_dw[jj], mode="conformer",
                        extra=(conf_b_dw[jj], conf_ln_g[jj], conf_ln_b[jj]), tt=tt)
            w_out = conf_w_out
        elif kind == 1:
            tn = _tile(512, d)
            bg, p = _inproj(x2, mod3, g_pre, sconv_w_in.astype(BF16), jj, [0, d // tn, 2 * d // tn],
                            d // tn, _epi_sconv, [F32, F32], tm=tm_in, tn=tn, seq_len=seq, name="sconv_in")
            a = _dwconv(p.reshape(bsz, seq, d), sconv_w_conv[jj], mode="sconv",
                        extra=(bg.reshape(bsz, seq, d),), tt=tt)
            w_out = sconv_w_out
        elif kind == 2:
            heads = MLSTM_HEADS
            qk, vd = d // 2, d
            w_main = mlstm_w_in[:, :, :2 * qk + 2 * vd].astype(BF16)
            w_if = mlstm_w_in[jj, :, 2 * qk + 2 * vd:]
            tn = _tile(1024, qk)
            epi = functools.partial(_epi_scale_first, n_first=qk // tn, scale=float((qk // heads) ** -0.5))
            qkvo, gt = _inproj(x2, mod3, g_pre, w_main, jj, [0], (2 * qk + 2 * vd) // tn, epi,
                               [BF16], tm=tm_in, tn=tn, seq_len=seq,
                               small_t=_pad_cols(w_if, GATE_ROWS).T, name="mlstm_in")
            a = _mlstm_rec(qkvo.reshape(bsz, seq, -1), gt, mlstm_b_if[jj], mlstm_norm_g[jj], heads=heads,
                           chunk=MLSTM_CHUNK)
            w_out = mlstm_w_out
        else:
            heads = GLA_HEADS
            kd, vd = d // 2, d
            w_main = gla_w_in[:, :, :2 * kd + 2 * vd].astype(BF16)
            w_al = gla_w_in[jj, :, 2 * kd + 2 * vd:]
            tn = _tile(1024, kd)
            epi = functools.partial(_epi_scale_first, n_first=kd // tn, scale=float((kd // heads) ** -0.5))
            qkvr, al_t = _inproj(x2, mod3, g_pre, w_main, jj, [0], (2 * kd + 2 * vd) // tn, epi,
                                 [BF16], tm=tm_in, tn=tn, seq_len=seq,
                                 small_t=_pad_cols(w_al, GATE_ROWS).T, name="gla_in")
            a = _gla_rec(qkvr.reshape(bsz, seq, -1), al_t, gla_w_alpha[jj],
                         gla_b_alpha[jj], gla_b_r[jj], gla_norm_g[jj], heads=heads, chunk=GLA_CHUNK)
            w_out = gla_w_out
        ffn_mod3 = mods[2 * i + 1].reshape(bsz, 1, 3 * d)
        x2, hn = _outproj(a.reshape(n, d), w_out.astype(BF16), jj, x2, mod3, g_post, tm=tm_out,
                          seq_len=seq, next_pre=(ffn_mod3, norm_pre[i, 1]), name="mixer_out")
        act = _ffn_in(hn, ffn_w_gate_up, i, tm=_tile(2048, n), tn=_tile(512, d_ff))
        x2 = _outproj(act, w_down, i, x2, ffn_mod3, norm_post[i, 1], tm=tm_out, seq_len=seq,
                      name="ffn_out")

    return x2.reshape(bsz, seq, d)
```

```python
import functools

import numpy as np
import jax
import jax.numpy as jnp
from jax import lax
from jax.experimental import pallas as pl
from jax.experimental.pallas import tpu as pltpu

F32 = jnp.float32
BF16 = jnp.bfloat16

RMS_EPS = 1e-6
LN_EPS = 1e-5
MLSTM_HEADS = 8
MLSTM_CHUNK = 128
GATE_SOFTCAP = 15.0
GLA_HEADS = 4
GLA_CHUNK = 64
GLA_TEMP = 16.0

LANE = 128
GATE_ROWS = 16
VMEM_LIMIT_BYTES = 56 * 1024 * 1024


def _tile(pref, dim):
    return pref if dim % pref == 0 else dim


def _params(*sem):
    return pltpu.CompilerParams(dimension_semantics=sem, vmem_limit_bytes=VMEM_LIMIT_BYTES)


def _sigmoid(z):
    return 1.0 / (1.0 + jnp.exp(-z))


def _silu(z):
    return z * _sigmoid(z)


def _log_sigmoid(z):
    return jnp.minimum(z, 0.0) - jnp.log(1.0 + jnp.exp(-jnp.abs(z)))


def _dot(a, b):
    return jnp.dot(a, b, preferred_element_type=F32)


def _dot_nt(a, b):
    return lax.dot_general(a, b, (((1,), (1,)), ((), ())), preferred_element_type=F32)


def _dot_tn(a, b):
    return lax.dot_general(a, b, (((0,), (0,)), ((), ())), preferred_element_type=F32)


def _split3(x):
    hi = x.astype(BF16)
    r1 = x - hi.astype(F32)
    mid = r1.astype(BF16)
    lo = (r1 - mid.astype(F32)).astype(BF16)
    return hi, mid, lo


def _tri(n, upper):
    r = lax.broadcasted_iota(jnp.int32, (n, n), 0)
    c = lax.broadcasted_iota(jnp.int32, (n, n), 1)
    return jnp.where((r <= c) if upper else (c <= r), 1.0, 0.0).astype(BF16)


def _cumsum_rows(x):
    tri = _tri(x.shape[0], upper=False)
    hi, mid, lo = _split3(x)
    return _dot(tri, hi) + _dot(tri, mid) + _dot(tri, lo)


def _cumsum_lanes(x):
    tri = _tri(x.shape[1], upper=True)
    hi, mid, lo = _split3(x)
    return _dot(hi, tri) + _dot(mid, tri) + _dot(lo, tri)


def _ada_body(c_ref, w_ref, b_ref, o_ref):
    sc = _silu(c_ref[...]).astype(BF16)
    o_ref[0] = _dot(sc, w_ref[0].astype(BF16)) + b_ref[0]


def _ada_mods(c, ada_w, ada_b):
    depth, two, d, d3 = ada_w.shape
    n_sub = depth * two
    bsz = c.shape[0]
    rows = -(-bsz // 16) * 16
    c_pad = jnp.pad(c, ((0, rows - bsz), (0, 0)))
    tn = _tile(1024, d3)
    out = pl.pallas_call(
        _ada_body,
        grid=(n_sub, d3 // tn),
        in_specs=[pl.BlockSpec((rows, d), lambda l, j: (0, 0)),
                  pl.BlockSpec((1, d, tn), lambda l, j: (l, 0, j)),
                  pl.BlockSpec((1, 1, tn), lambda l, j: (l, 0, j))],
        out_specs=pl.BlockSpec((1, rows, tn), lambda l, j: (l, 0, j)),
        out_shape=jax.ShapeDtypeStruct((n_sub, rows, d3), F32),
        compiler_params=_params("arbitrary", "arbitrary"),
        name="ada_mods",
    )(c_pad, ada_w.reshape(n_sub, d, d3), ada_b.reshape(n_sub, 1, d3))
    return out[:, :bsz, :]


def _inproj_body(*refs, n_parts, n_out, epi, with_small_t, row_chunk):
    x_ref, shift_ref, scale_ref, gpre_ref = refs[:4]
    w_refs = refs[4:4 + n_parts]
    pos = 4 + n_parts
    wst_ref = gt_ref = None
    if with_small_t:
        wst_ref = refs[pos]
        pos += 1
    out_refs = refs[pos:pos + n_out]
    pos += n_out
    if with_small_t:
        gt_ref = refs[pos]
        pos += 1
    hn_ref = refs[pos]

    j = pl.program_id(1)
    tm = x_ref.shape[0]

    @pl.when(j == 0)
    def _():
        gain = gpre_ref[...] * (1.0 + scale_ref[0])
        shift = shift_ref[0]

        def chunk(r, carry):
            rows = pl.ds(pl.multiple_of(r * row_chunk, row_chunk), row_chunk)
            x = x_ref[rows, :]
            h = x * lax.rsqrt(jnp.mean(x * x, axis=-1, keepdims=True) + RMS_EPS) * gain + shift
            hn_ref[rows, :] = h.astype(BF16)
            return carry
        lax.fori_loop(0, tm // row_chunk, chunk, 0, unroll=4)
        if with_small_t:
            gt_ref[...] = _dot_nt(wst_ref[...].astype(BF16), hn_ref[...])

    hn = hn_ref[...]
    accs = [_dot(hn, w_ref[...].astype(BF16)) for w_ref in w_refs]
    outs = epi(accs, j)
    for o_ref, o in zip(out_refs, outs):
        o_ref[...] = o.astype(o_ref.dtype)


def _inproj(x2, mod3, g_pre, w, layer, part_offsets, n_col_tiles, epi, out_defs, *, tm, tn, seq_len,
            small_t=None, name="inproj"):
    n, d = x2.shape
    tiles_per_seq = seq_len // tm
    n_parts = len(part_offsets)
    in_specs = [pl.BlockSpec((tm, d), lambda i, j: (i, 0)),
                pl.BlockSpec((1, 1, d), lambda i, j: (i // tiles_per_seq, 0, 0)),
                pl.BlockSpec((1, 1, d), lambda i, j: (i // tiles_per_seq, 0, 1)),
                pl.BlockSpec((1, d), lambda i, j: (0, 0))]
    args = [x2, mod3, mod3, g_pre.reshape(1, d)]
    for off in part_offsets:
        in_specs.append(pl.BlockSpec((None, d, tn), lambda i, j, off=off: (layer, 0, off + j)))
        args.append(w)
    if small_t is not None:
        in_specs.append(pl.BlockSpec((GATE_ROWS, d), lambda i, j: (0, 0)))
        args.append(small_t)
    out_specs = [pl.BlockSpec((tm, tn), lambda i, j: (i, j)) for _ in out_defs]
    out_shape = [jax.ShapeDtypeStruct((n, n_col_tiles * tn), dt) for dt in out_defs]
    if small_t is not None:
        out_specs.append(pl.BlockSpec((GATE_ROWS, tm), lambda i, j: (0, i)))
        out_shape.append(jax.ShapeDtypeStruct((GATE_ROWS, n), F32))
    body = functools.partial(_inproj_body, n_parts=n_parts, n_out=len(out_defs), epi=epi,
                             with_small_t=small_t is not None, row_chunk=min(32, tm))
    return pl.pallas_call(
        body,
        grid=(n // tm, n_col_tiles),
        in_specs=in_specs,
        out_specs=out_specs,
        out_shape=out_shape,
        scratch_shapes=[pltpu.VMEM((tm, d), BF16)],
        compiler_params=_params("arbitrary", "arbitrary"),
        name=name,
    )(*args)


def _outproj_body(*refs, row_chunk, with_next):
    if with_next:
        a_ref, w_ref, x_ref, gate_ref, gpost_ref, nshift_ref, nscale_ref, ngpre_ref, o_ref, hn_ref = refs
    else:
        a_ref, w_ref, x_ref, gate_ref, gpost_ref, o_ref = refs
    o_ref[...] = _dot(a_ref[...], w_ref[...])
    tm = o_ref.shape[0]

    gain = gpost_ref[...] * gate_ref[0]
    if with_next:
        next_gain = ngpre_ref[...] * (1.0 + nscale_ref[0])
        next_shift = nshift_ref[0]

    for r in range(tm // row_chunk):
        rows = slice(r * row_chunk, (r + 1) * row_chunk)
        y = o_ref[rows, :]
        xn = x_ref[rows, :] + y * lax.rsqrt(jnp.mean(y * y, axis=-1, keepdims=True) + RMS_EPS) * gain
        o_ref[rows, :] = xn
        if with_next:
            h = xn * lax.rsqrt(jnp.mean(xn * xn, axis=-1, keepdims=True) + RMS_EPS) * next_gain + next_shift
            hn_ref[rows, :] = h.astype(BF16)


def _outproj(a, w, layer, x2, mod3, g_post, *, tm, seq_len, next_pre=None, name="outproj"):
    n, kdim = a.shape
    d = w.shape[2]
    tiles_per_seq = seq_len // tm
    with_next = next_pre is not None
    body = functools.partial(_outproj_body, row_chunk=min(32, tm), with_next=with_next)
    in_specs = [pl.BlockSpec((tm, kdim), lambda i: (i, 0)),
                pl.BlockSpec((None, kdim, d), lambda i: (layer, 0, 0), pipeline_mode=pl.Buffered(1)),
                pl.BlockSpec((tm, d), lambda i: (i, 0)),
                pl.BlockSpec((1, 1, d), lambda i: (i // tiles_per_seq, 0, 2)),
                pl.BlockSpec((1, d), lambda i: (0, 0))]
    args = [a, w, x2, mod3, g_post.reshape(1, d)]
    out_specs = [pl.BlockSpec((tm, d), lambda i: (i, 0))]
    out_shape = [jax.ShapeDtypeStruct((n, d), F32)]
    if with_next:
        next_mod3, next_g_pre = next_pre
        in_specs += [pl.BlockSpec((1, 1, d), lambda i: (i // tiles_per_seq, 0, 0)),
                     pl.BlockSpec((1, 1, d), lambda i: (i // tiles_per_seq, 0, 1)),
                     pl.BlockSpec((1, d), lambda i: (0, 0))]
        args += [next_mod3, next_mod3, next_g_pre.reshape(1, d)]
        out_specs.append(pl.BlockSpec((tm, d), lambda i: (i, 0)))
        out_shape.append(jax.ShapeDtypeStruct((n, d), BF16))
    outs = pl.pallas_call(
        body,
        grid=(n // tm,),
        in_specs=in_specs,
        out_specs=out_specs,
        out_shape=out_shape,
        compiler_params=_params("arbitrary"),
        name=name,
    )(*args)
    return outs if with_next else outs[0]


def _ffn_in_body(hn_ref, wg_ref, wu_ref, o_ref):
    hn = hn_ref[...]
    gate = _dot(hn, wg_ref[...].astype(BF16))
    up = _dot(hn, wu_ref[...].astype(BF16))
    o_ref[...] = (_silu(gate) * up).astype(o_ref.dtype)


def _ffn_in(hn, w, layer, *, tm, tn):
    n, d = hn.shape
    d_ff = w.shape[2] // 2
    nj = d_ff // tn
    return pl.pallas_call(
        _ffn_in_body,
        grid=(nj, n // tm),
        in_specs=[pl.BlockSpec((tm, d), lambda j, i: (i, 0)),
                  pl.BlockSpec((None, d, tn), lambda j, i: (layer, 0, j)),
                  pl.BlockSpec((None, d, tn), lambda j, i: (layer, 0, nj + j))],
        out_specs=pl.BlockSpec((tm, tn), lambda j, i: (i, j)),
        out_shape=jax.ShapeDtypeStruct((n, d_ff), BF16),
        compiler_params=_params("arbitrary", "arbitrary"),
        name="ffn_in",
    )(hn, w, w)


def _dwconv_body(*refs, width, halo, col_chunk, row_tile, mode):
    if mode == "conformer":
        cur_ref, halo_ref, w_ref, b_ref, lng_ref, lnb_ref, o_ref, win_ref, sh_ref, y_ref = refs
    else:
        cur_ref, halo_ref, w_ref, gatein_ref, o_ref, win_ref, sh_ref, y_ref = refs
    t = pl.program_id(1)
    tt, d = y_ref.shape
    off = halo - (width - 1)
    sh_rows = sh_ref.shape[1]

    win_ref[0:halo, :] = jnp.where(t == 0, 0.0, halo_ref[0])
    win_ref[halo:halo + tt, :] = cur_ref[0]

    def col_body(c, carry):
        cs = pl.ds(pl.multiple_of(c * col_chunk, col_chunk), col_chunk)
        sh_ref[0] = win_ref[:, cs]
        for s in range(1, 8):
            sh_ref[s, 0:sh_rows - 8, :] = win_ref[pl.ds(s, sh_rows - 8), cs]
        for r in range(tt // row_tile):
            acc = jnp.zeros((row_tile, col_chunk), F32)
            for k in range(width):
                phase = (off + k) % 8
                base = off + k - phase
                acc = acc + w_ref[k:k + 1, cs] * sh_ref[phase, pl.ds(r * row_tile + base, row_tile), :]
            y_ref[pl.ds(r * row_tile, row_tile), cs] = acc
        return carry
    lax.fori_loop(0, d // col_chunk, col_body, 0)

    def tail(r, carry):
        rows = pl.ds(pl.multiple_of(r * 16, 16), 16)
        if mode == "conformer":
            u = y_ref[rows, :] + b_ref[...]
            mu = jnp.mean(u, axis=-1, keepdims=True)
            var = jnp.mean(jnp.square(u - mu), axis=-1, keepdims=True)
            un = (u - mu) * lax.rsqrt(var + LN_EPS) * lng_ref[...] + lnb_ref[...]
            o_ref[0, rows, :] = _silu(un).astype(o_ref.dtype)
        else:
            o_ref[0, rows, :] = (gatein_ref[0, rows, :] * y_ref[rows, :]).astype(o_ref.dtype)
        return carry
    lax.fori_loop(0, tt // 16, tail, 0, unroll=4)


def _dwconv(u3, w, *, mode, extra, tt):
    bsz, seq, d = u3.shape
    width = w.shape[0]
    halo = 8 * (-(-(width - 1) // 8))
    w_pad = jnp.pad(w, ((0, halo - width), (0, 0)))
    tiles = tt // halo
    in_specs = [pl.BlockSpec((1, tt, d), lambda b, t: (b, t, 0)),
                pl.BlockSpec((1, halo, d), lambda b, t: (b, jnp.maximum(t * tiles - 1, 0), 0)),
                pl.BlockSpec((halo, d), lambda b, t: (0, 0))]
    args = [u3, u3, w_pad]
    if mode == "conformer":
        for v in extra:
            in_specs.append(pl.BlockSpec((1, d), lambda b, t: (0, 0)))
            args.append(v.reshape(1, d))
    else:
        in_specs.append(pl.BlockSpec((1, tt, d), lambda b, t: (b, t, 0)))
        args.append(extra[0])
    col_chunk = min(256, d)
    body = functools.partial(_dwconv_body, width=width, halo=halo, col_chunk=col_chunk,
                             row_tile=min(64, tt), mode=mode)
    return pl.pallas_call(
        body,
        grid=(bsz, seq // tt),
        in_specs=in_specs,
        out_specs=pl.BlockSpec((1, tt, d), lambda b, t: (b, t, 0)),
        out_shape=jax.ShapeDtypeStruct((bsz, seq, d), BF16),
        scratch_shapes=[pltpu.VMEM((halo + tt, d), F32),
                        pltpu.VMEM((8, halo + tt, col_chunk), F32),
                        pltpu.VMEM((tt, d), F32)],
        compiler_params=_params("arbitrary", "arbitrary"),
        name="dwconv_" + mode,
    )(*args)


def _split3_f32(x):
    hi = x.astype(BF16).astype(F32)
    r1 = x - hi
    mid = r1.astype(BF16).astype(F32)
    lo = (r1 - mid).astype(BF16).astype(F32)
    return hi, mid, lo


def _mlstm_body(q_ref, k_ref, v_ref, o_ref, gt_ref, bcol_ref, ng_ref, out_ref, c_ref, m_ref, *, heads,
                head_group_size):
    c = pl.program_id(1)
    L = q_ref.shape[1]
    dk = q_ref.shape[2] // heads
    dv = v_ref.shape[2] // heads
    rep = dv // L

    @pl.when(c == 0)
    def _():
        c_ref[...] = jnp.zeros_like(c_ref)
        m_ref[...] = jnp.zeros_like(m_ref)

    g = gt_ref[...] + bcol_ref[...]
    g = GATE_SOFTCAP * jnp.tanh(g / GATE_SOFTCAP)
    cum = _cumsum_lanes(_log_sigmoid(g))
    li = g[0:heads]
    b = cum[heads:2 * heads]
    r = li - b
    lane = lax.broadcasted_iota(jnp.int32, (heads, L), 1)
    pm = r
    shift = 1
    while shift < L:
        pm = jnp.maximum(pm, jnp.where(lane >= shift, pltpu.roll(pm, shift, 1), -jnp.inf))
        shift *= 2
    m_prev = m_ref[...]
    u = jnp.maximum(m_prev, pm)
    u_last = jnp.broadcast_to(u[:, L - 1:L], (heads, L))
    b_last = jnp.broadcast_to(b[:, L - 1:L], (heads, L))
    m_ref[...] = b_last + u_last
    decay = jnp.exp(m_prev - u_last)
    lhs_terms = _split3_f32(-u) + _split3_f32(m_prev - u) + _split3_f32(-(b + u)) + _split3_f32(r - u_last)
    lhs_rows = (0, 1, 2, 6, 7, 8, 9, 10, 11, 12, 13, 14)
    r_terms = _split3_f32(r)

    rid = lax.broadcasted_iota(jnp.int32, (GATE_ROWS, L), 0)
    lhs_const = jnp.where((rid >= 3) & (rid < 6), 1.0, 0.0)
    rhs_blocks = [jnp.where((rid >= lo) & (rid < lo + 3), 1.0, 0.0).astype(BF16) for lo in (6, 9, 12)]
    rhs0_const = jnp.where(rid < 3, 1.0, 0.0)

    row = lax.broadcasted_iota(jnp.int32, (L, L), 0)
    col = lax.broadcasted_iota(jnp.int32, (L, L), 1)
    causal = col <= row
    ones_ll = jnp.ones((L, L), BF16)
    ones_dv = jnp.ones((dv, L), BF16)

    def operands(h):
        lhs = lhs_const
        for i, term in zip(lhs_rows, lhs_terms):
            lhs = jnp.where(rid == i, term[h:h + 1, :], lhs)
        rhs0 = rhs0_const
        for i, term in zip((3, 4, 5), r_terms):
            rhs0 = jnp.where(rid == i, term[h:h + 1, :], rhs0)
        return lhs.astype(BF16), jnp.concatenate([rhs0.astype(BF16)] + rhs_blocks, axis=1)

    def mean_sq(x):
        xx = x * x
        hi = xx.astype(BF16)
        lo = (xx - hi.astype(F32)).astype(BF16)
        return (_dot(hi, ones_dv) + _dot(lo, ones_dv)) * (1.0 / dv)

    def head_group(H):
        qs = {h: q_ref[0, :, h * dk:(h + 1) * dk] for h in H}
        ks = {h: k_ref[0, :, h * dk:(h + 1) * dk] for h in H}
        vs = {h: v_ref[0, :, h * dv:(h + 1) * dv] for h in H}
        caugs = {h: c_ref[h] for h in H}
        zs = {h: _dot_tn(*operands(h)) for h in H}
        qks = {h: _dot_nt(qs[h], ks[h]) for h in H}
        qcs = {h: _dot(qs[h], caugs[h].astype(BF16)) for h in H}
        ps = {h: jnp.exp(jnp.where(causal, zs[h][:, 0:L], -jnp.inf)) for h in H}
        ss = {h: (qks[h] * ps[h]).astype(BF16) for h in H}
        svs = {h: _dot(ss[h], jnp.concatenate([vs[h], ones_ll], axis=1)) for h in H}
        nds = {h: svs[h] + jnp.concatenate([jnp.exp(zs[h][:, L:2 * L])] * (rep + 1), axis=1) * qcs[h]
               for h in H}
        scales = {h: 1.0 / jnp.maximum(jnp.abs(nds[h][:, dv:]), jnp.exp(zs[h][:, 2 * L:3 * L])) for h in H}
        houts = {h: nds[h][:, :dv] * jnp.concatenate([scales[h]] * rep, axis=1) for h in H}

        for h in H:
            wk = jnp.exp(zs[h][:, 3 * L:4 * L])
            vw = jnp.concatenate([vs[h].astype(F32) * jnp.concatenate([wk] * rep, axis=1), wk], axis=1)
            dec = jnp.concatenate([jnp.broadcast_to(decay[h:h + 1, :], (dk, L))] * (rep + 1), axis=1)
            c_ref[h] = dec * caugs[h] + _dot_tn(ks[h], vw.astype(BF16))

        mss = {h: mean_sq(houts[h]) for h in H}
        for h in H:
            hn = houts[h] * jnp.concatenate([lax.rsqrt(mss[h] + RMS_EPS)] * rep, axis=1)
            hn = hn * ng_ref[:, h * dv:(h + 1) * dv]
            og = _sigmoid(o_ref[0, :, h * dv:(h + 1) * dv].astype(F32))
            out_ref[0, :, h * dv:(h + 1) * dv] = (og * hn).astype(out_ref.dtype)

    for g0 in range(0, heads, head_group_size):
        head_group(range(g0, min(g0 + head_group_size, heads)))


def _mlstm_rec(qkvo3, gt, b_if, norm_g, *, heads, chunk):
    bsz, seq, _ = qkvo3.shape
    dvt = norm_g.shape[0]
    dkt = dvt // 2
    nc = seq // chunk
    assert (dvt // heads) % chunk == 0 and 2 * heads <= GATE_ROWS
    bcol = jnp.pad(b_if, (0, GATE_ROWS - b_if.shape[0])).reshape(GATE_ROWS, 1)
    body = functools.partial(_mlstm_body, heads=heads, head_group_size=4)
    return pl.pallas_call(
        body,
        grid=(bsz, nc),
        in_specs=[pl.BlockSpec((1, chunk, dkt), lambda b, c: (b, c, 0)),
                  pl.BlockSpec((1, chunk, dkt), lambda b, c: (b, c, 1)),
                  pl.BlockSpec((1, chunk, dvt), lambda b, c: (b, c, 1)),
                  pl.BlockSpec((1, chunk, dvt), lambda b, c: (b, c, 2)),
                  pl.BlockSpec((GATE_ROWS, chunk), lambda b, c: (0, b * nc + c)),
                  pl.BlockSpec((GATE_ROWS, 1), lambda b, c: (0, 0)),
                  pl.BlockSpec((1, dvt), lambda b, c: (0, 0))],
        out_specs=pl.BlockSpec((1, chunk, dvt), lambda b, c: (b, c, 0)),
        out_shape=jax.ShapeDtypeStruct((bsz, seq, dvt), BF16),
        scratch_shapes=[pltpu.VMEM((heads, dkt // heads, dvt // heads + chunk), F32),
                        pltpu.VMEM((heads, chunk), F32)],
        compiler_params=_params("arbitrary", "arbitrary"),
        name="mlstm_rec",
    )(qkvo3, qkvo3, qkvo3, qkvo3, gt, bcol, norm_g.reshape(1, dvt))


def _gla_level_masks(chunk):
    t = np.arange(chunk)[:, None]
    s = np.arange(chunk)[None, :]
    masks = []
    b = chunk // 2
    while b >= 1:
        masks.append((t // (2 * b) == s // (2 * b)) & ((t % (2 * b)) >= b) & ((s % (2 * b)) < b))
        b //= 2
    return jnp.asarray(np.stack(masks), dtype=F32)


def _mid_rows(bc, b, row):
    L, dk = bc.shape
    if b >= 4:
        return jnp.concatenate([jnp.broadcast_to(bc[m:m + 1, :], (2 * b, dk)) for m in range(b, L, 2 * b)],
                               axis=0)
    up1 = pltpu.roll(bc, L - 1, 0)
    if b == 1:
        return jnp.where((row & 1) == 0, up1, bc)
    up2 = pltpu.roll(bc, L - 2, 0)
    down1 = pltpu.roll(bc, 1, 0)
    r4 = row & 3
    return jnp.where(r4 == 0, up2, jnp.where(r4 == 1, up1, jnp.where(r4 == 2, bc, down1)))


def _gla_body(q_ref, k_ref, v_ref, r_ref, al_ref, wa_ref, ba_ref, br_ref, ng_ref, mask_ref,
              out_ref, st_ref, *, heads):
    c = pl.program_id(1)
    L = q_ref.shape[1]
    dk = q_ref.shape[2] // heads
    dv = v_ref.shape[2] // heads
    n_levels = mask_ref.shape[0]

    @pl.when(c == 0)
    def _():
        st_ref[...] = jnp.zeros_like(st_ref)

    al = jnp.where(lax.rem(c, 2) == 0, al_ref[:, 0:L], al_ref[:, L:2 * L])
    z = _dot_tn(al.astype(BF16), wa_ref[...]) + ba_ref[...]
    log_alpha = _log_sigmoid(z) / GLA_TEMP
    bc_all = _cumsum_rows(log_alpha)

    eye = (lax.broadcasted_iota(jnp.int32, (L, L), 0) == lax.broadcasted_iota(jnp.int32, (L, L), 1))
    row = lax.broadcasted_iota(jnp.int32, (L, dk), 0)

    H = range(heads)
    qfs = [q_ref[0, :, h * dk:(h + 1) * dk].astype(F32) for h in H]
    kfs = [k_ref[0, :, h * dk:(h + 1) * dk].astype(F32) for h in H]
    vs = [v_ref[0, :, h * dv:(h + 1) * dv] for h in H]
    bcs = [bc_all[:, h * dk:(h + 1) * dk] for h in H]
    sts = [st_ref[h] for h in H]

    os_ = [_dot_nt((qfs[h] * jnp.exp(bcs[h])).astype(BF16), sts[h].astype(BF16)) for h in H]

    accs = [jnp.where(eye, jnp.sum(qfs[h] * kfs[h], axis=-1, keepdims=True), 0.0) for h in H]
    for i in range(n_levels):
        b = L >> (i + 1)
        upper = (row & (2 * b - 1)) >= b
        cls = [(jnp.where(upper, qfs[h], kfs[h])
                * jnp.exp(-jnp.abs(bcs[h] - _mid_rows(bcs[h], b, row)))).astype(BF16) for h in H]
        accs = [accs[h] + mask_ref[i] * _dot_nt(cls[h], cls[h]) for h in H]
    os_ = [os_[h] + _dot(accs[h].astype(BF16), vs[h]) for h in H]

    for h in H:
        b_last = bcs[h][L - 1:L, :]
        k_up = (kfs[h] * jnp.exp(jnp.minimum(b_last - bcs[h], 0.0))).astype(BF16)
        st_ref[h] = sts[h] * jnp.exp(b_last) + _dot_tn(vs[h], k_up)

    for h in H:
        o = os_[h]
        on = o * lax.rsqrt(jnp.mean(o * o, axis=-1, keepdims=True) + RMS_EPS)
        on = on * ng_ref[:, h * dv:(h + 1) * dv]
        rg = r_ref[0, :, h * dv:(h + 1) * dv].astype(F32) + br_ref[:, h * dv:(h + 1) * dv]
        out_ref[0, :, h * dv:(h + 1) * dv] = (on * _silu(rg)).astype(out_ref.dtype)


def _gla_rec(qkvr3, al_t, w_alpha, b_alpha, b_r, norm_g, *, heads, chunk):
    bsz, seq, _ = qkvr3.shape
    dvt = norm_g.shape[0]
    dkt = dvt // 2
    nc = seq // chunk
    rank = w_alpha.shape[0]
    assert rank <= GATE_ROWS and nc % 2 == 0
    wa = jnp.pad(w_alpha, ((0, GATE_ROWS - rank), (0, 0))).astype(BF16)
    masks = _gla_level_masks(chunk)
    body = functools.partial(_gla_body, heads=heads)
    return pl.pallas_call(
        body,
        grid=(bsz, seq // chunk),
        in_specs=[pl.BlockSpec((1, chunk, dkt), lambda b, c: (b, c, 0)),
                  pl.BlockSpec((1, chunk, dkt), lambda b, c: (b, c, 1)),
                  pl.BlockSpec((1, chunk, dvt), lambda b, c: (b, c, 1)),
                  pl.BlockSpec((1, chunk, dvt), lambda b, c: (b, c, 2)),
                  pl.BlockSpec((GATE_ROWS, 2 * chunk), lambda b, c: (0, (b * nc + c) // 2)),
                  pl.BlockSpec((GATE_ROWS, dkt), lambda b, c: (0, 0)),
                  pl.BlockSpec((1, dkt), lambda b, c: (0, 0)),
                  pl.BlockSpec((1, dvt), lambda b, c: (0, 0)),
                  pl.BlockSpec((1, dvt), lambda b, c: (0, 0)),
                  pl.BlockSpec(masks.shape, lambda b, c: (0, 0, 0))],
        out_specs=pl.BlockSpec((1, chunk, dvt), lambda b, c: (b, c, 0)),
        out_shape=jax.ShapeDtypeStruct((bsz, seq, dvt), BF16),
        scratch_shapes=[pltpu.VMEM((heads, dvt // heads, dkt // heads), F32)],
        compiler_params=_params("arbitrary", "arbitrary"),
        name="gla_rec",
    )(qkvr3, qkvr3, qkvr3, qkvr3, al_t, wa, b_alpha.reshape(1, dkt), b_r.reshape(1, dvt),
      norm_g.reshape(1, dvt), masks)


def _epi_glu(accs, j):
    a, g = accs
    return [a * _sigmoid(g)]


def _epi_sconv(accs, j):
    bg, cg, xv = accs
    return [bg, cg * xv]


def _epi_scale_first(accs, j, *, n_first, scale):
    (a,) = accs
    return [a * jnp.where(j < n_first, scale, 1.0)]


def _pad_cols(w, cols):
    return jnp.pad(w, ((0, 0), (0, cols - w.shape[1])))


def kernel(x, c, conf_w_in, conf_w_dw, conf_b_dw, conf_ln_g, conf_ln_b, conf_w_out, sconv_w_in, sconv_w_conv, sconv_w_out, mlstm_w_in, mlstm_b_if, mlstm_norm_g, mlstm_w_out, gla_w_in, gla_w_alpha, gla_b_alpha, gla_b_r, gla_norm_g, gla_w_out, ada_w, ada_b, norm_pre, norm_post, ffn_w_gate_up, ffn_w_down):
    bsz, seq, d = x.shape
    n = bsz * seq
    depth = ada_w.shape[0]
    d_ff = ffn_w_down.shape[1]
    tm_in = _tile(1024, seq)
    tm_out = _tile(512, seq)
    tt = _tile(512, seq)

    mods = _ada_mods(c, ada_w, ada_b)
    x2 = x.reshape(n, d)
    w_down = ffn_w_down.astype(BF16)

    for i in range(depth):
        kind, jj = i % 4, i // 4
        mod3 = mods[2 * i].reshape(bsz, 1, 3 * d)
        g_pre, g_post = norm_pre[i, 0], norm_post[i, 0]
        if kind == 0:
            tn = _tile(512, d)
            (u,) = _inproj(x2, mod3, g_pre, conf_w_in, jj, [0, d // tn], d // tn, _epi_glu,
                           [F32], tm=tm_in, tn=tn, seq_len=seq, name="conf_in")
            a = _dwconv(u.reshape(bsz, seq, d), conf_w_dw[jj], mode="conformer",
                        extra=(conf_b_dw[jj], conf_ln_g[jj], conf_ln_b[jj]), tt=tt)
            w_out = conf_w_out
        elif kind == 1:
            tn = _tile(512, d)
            bg, p = _inproj(x2, mod3, g_pre, sconv_w_in.astype(BF16), jj, [0, d // tn, 2 * d // tn],
                            d // tn, _epi_sconv, [F32, F32], tm=tm_in, tn=tn, seq_len=seq, name="sconv_in")
            a = _dwconv(p.reshape(bsz, seq, d), sconv_w_conv[jj], mode="sconv",
                        extra=(bg.reshape(bsz, seq, d),), tt=tt)
            w_out = sconv_w_out
        elif kind == 2:
            heads = MLSTM_HEADS
            qk, vd = d // 2, d
            w_main = mlstm_w_in[:, :, :2 * qk + 2 * vd].astype(BF16)
            w_if = mlstm_w_in[jj, :, 2 * qk + 2 * vd:]
            tn = _tile(1024, qk)
            epi = functools.partial(_epi_scale_first, n_first=qk // tn, scale=float((qk // heads) ** -0.5))
            qkvo, gt = _inproj(x2, mod3, g_pre, w_main, jj, [0], (2 * qk + 2 * vd) // tn, epi,
                               [BF16], tm=tm_in, tn=tn, seq_len=seq,
                               small_t=_pad_cols(w_if, GATE_ROWS).T, name="mlstm_in")
            a = _mlstm_rec(qkvo.reshape(bsz, seq, -1), gt, mlstm_b_if[jj], mlstm_norm_g[jj], heads=heads,
                           chunk=MLSTM_CHUNK)
            w_out = mlstm_w_out
        else:
            heads = GLA_HEADS
            kd, vd = d // 2, d
            w_main = gla_w_in[:, :, :2 * kd + 2 * vd].astype(BF16)
            w_al = gla_w_in[jj, :, 2 * kd + 2 * vd:]
            tn = _tile(1024, kd)
            epi = functools.partial(_epi_scale_first, n_first=kd // tn, scale=float((kd // heads) ** -0.5))
            qkvr, al_t = _inproj(x2, mod3, g_pre, w_main, jj, [0], (2 * kd + 2 * vd) // tn, epi,
                                 [BF16], tm=tm_in, tn=tn, seq_len=seq,
                                 small_t=_pad_cols(w_al, GATE_ROWS).T, name="gla_in")
            a = _gla_rec(qkvr.reshape(bsz, seq, -1), al_t, gla_w_alpha[jj],
                         gla_b_alpha[jj], gla_b_r[jj], gla_norm_g[jj], heads=heads, chunk=GLA_CHUNK)
            w_out = gla_w_out
        ffn_mod3 = mods[2 * i + 1].reshape(bsz, 1, 3 * d)
        x2, hn = _outproj(a.reshape(n, d), w_out.astype(BF16), jj, x2, mod3, g_post, tm=tm_out,
                          seq_len=seq, next_pre=(ffn_mod3, norm_pre[i, 1]), name="mixer_out")
        act = _ffn_in(hn, ffn_w_gate_up, i, tm=_tile(2048, n), tn=_tile(512, d_ff))
        x2 = _outproj(act, w_down, i, x2, ffn_mod3, norm_post[i, 1], tm=tm_out, seq_len=seq,
                      name="ffn_out")

    return x2.reshape(bsz, seq, d)
```

```python
import functools

import numpy as np
import jax
import jax.numpy as jnp
from jax import lax
from jax.experimental import pallas as pl
from jax.experimental.pallas import tpu as pltpu

F32 = jnp.float32
BF16 = jnp.bfloat16

RMS_EPS = 1e-6
LN_EPS = 1e-5
MLSTM_HEADS = 8
MLSTM_CHUNK = 128
GATE_SOFTCAP = 15.0
GLA_HEADS = 4
GLA_CHUNK = 128
GLA_TEMP = 16.0

LANE = 128
GATE_ROWS = 16
VMEM_LIMIT_BYTES = 56 * 1024 * 1024


def _tile(pref, dim):
    return pref if dim % pref == 0 else dim


def _params(*sem):
    return pltpu.CompilerParams(dimension_semantics=sem, vmem_limit_bytes=VMEM_LIMIT_BYTES)


def _sigmoid(z):
    return 1.0 / (1.0 + jnp.exp(-z))


def _silu(z):
    return z * _sigmoid(z)


def _log_sigmoid(z):
    return jnp.minimum(z, 0.0) - jnp.log(1.0 + jnp.exp(-jnp.abs(z)))


def _dot(a, b):
    return jnp.dot(a, b, preferred_element_type=F32)


def _dot_nt(a, b):
    return lax.dot_general(a, b, (((1,), (1,)), ((), ())), preferred_element_type=F32)


def _dot_tn(a, b):
    return lax.dot_general(a, b, (((0,), (0,)), ((), ())), preferred_element_type=F32)


def _split3(x):
    hi = x.astype(BF16)
    r1 = x - hi.astype(F32)
    mid = r1.astype(BF16)
    lo = (r1 - mid.astype(F32)).astype(BF16)
    return hi, mid, lo


def _tri(n, upper):
    r = lax.broadcasted_iota(jnp.int32, (n, n), 0)
    c = lax.broadcasted_iota(jnp.int32, (n, n), 1)
    return jnp.where((r <= c) if upper else (c <= r), 1.0, 0.0).astype(BF16)


def _cumsum_rows(x):
    tri = _tri(x.shape[0], upper=False)
    hi, mid, lo = _split3(x)
    return _dot(tri, hi) + _dot(tri, mid) + _dot(tri, lo)


def _cumsum_lanes(x):
    tri = _tri(x.shape[1], upper=True)
    hi, mid, lo = _split3(x)
    return _dot(hi, tri) + _dot(mid, tri) + _dot(lo, tri)


def _ada_body(c_ref, w_ref, b_ref, o_ref):
    sc = _silu(c_ref[...]).astype(BF16)
    o_ref[0] = _dot(sc, w_ref[0].astype(BF16)) + b_ref[0]


def _ada_mods(c, ada_w, ada_b):
    depth, two, d, d3 = ada_w.shape
    n_sub = depth * two
    bsz = c.shape[0]
    rows = -(-bsz // 16) * 16
    c_pad = jnp.pad(c, ((0, rows - bsz), (0, 0)))
    tn = _tile(1024, d3)
    out = pl.pallas_call(
        _ada_body,
        grid=(n_sub, d3 // tn),
        in_specs=[pl.BlockSpec((rows, d), lambda l, j: (0, 0)),
                  pl.BlockSpec((1, d, tn), lambda l, j: (l, 0, j)),
                  pl.BlockSpec((1, 1, tn), lambda l, j: (l, 0, j))],
        out_specs=pl.BlockSpec((1, rows, tn), lambda l, j: (l, 0, j)),
        out_shape=jax.ShapeDtypeStruct((n_sub, rows, d3), F32),
        compiler_params=_params("arbitrary", "arbitrary"),
        name="ada_mods",
    )(c_pad, ada_w.reshape(n_sub, d, d3), ada_b.reshape(n_sub, 1, d3))
    return out[:, :bsz, :]


def _inproj_body(*refs, n_parts, n_out, epi, with_small_t, row_chunk):
    x_ref, shift_ref, scale_ref, gpre_ref = refs[:4]
    w_refs = refs[4:4 + n_parts]
    pos = 4 + n_parts
    wst_ref = gt_ref = None
    if with_small_t:
        wst_ref = refs[pos]
        pos += 1
    out_refs = refs[pos:pos + n_out]
    pos += n_out
    if with_small_t:
        gt_ref = refs[pos]
        pos += 1
    hn_ref = refs[pos]

    j = pl.program_id(1)
    tm = x_ref.shape[0]

    @pl.when(j == 0)
    def _():
        gain = gpre_ref[...] * (1.0 + scale_ref[0])
        shift = shift_ref[0]

        def chunk(r, carry):
            rows = pl.ds(pl.multiple_of(r * row_chunk, row_chunk), row_chunk)
            x = x_ref[rows, :]
            h = x * lax.rsqrt(jnp.mean(x * x, axis=-1, keepdims=True) + RMS_EPS) * gain + shift
            hn_ref[rows, :] = h.astype(BF16)
            return carry
        lax.fori_loop(0, tm // row_chunk, chunk, 0, unroll=4)
        if with_small_t:
            gt_ref[...] = _dot_nt(wst_ref[...].astype(BF16), hn_ref[...])

    hn = hn_ref[...]
    accs = [_dot(hn, w_ref[...].astype(BF16)) for w_ref in w_refs]
    outs = epi(accs, j)
    for o_ref, o in zip(out_refs, outs):
        o_ref[...] = o.astype(o_ref.dtype)


def _inproj(x2, mod3, g_pre, w, layer, part_offsets, n_col_tiles, epi, out_defs, *, tm, tn, seq_len,
            small_t=None, name="inproj"):
    n, d = x2.shape
    tiles_per_seq = seq_len // tm
    n_parts = len(part_offsets)
    in_specs = [pl.BlockSpec((tm, d), lambda i, j: (i, 0)),
                pl.BlockSpec((1, 1, d), lambda i, j: (i // tiles_per_seq, 0, 0)),
                pl.BlockSpec((1, 1, d), lambda i, j: (i // tiles_per_seq, 0, 1)),
                pl.BlockSpec((1, d), lambda i, j: (0, 0))]
    args = [x2, mod3, mod3, g_pre.reshape(1, d)]
    for off in part_offsets:
        in_specs.append(pl.BlockSpec((None, d, tn), lambda i, j, off=off: (layer, 0, off + j)))
        args.append(w)
    if small_t is not None:
        in_specs.append(pl.BlockSpec((GATE_ROWS, d), lambda i, j: (0, 0)))
        args.append(small_t)
    out_specs = [pl.BlockSpec((tm, tn), lambda i, j: (i, j)) for _ in out_defs]
    out_shape = [jax.ShapeDtypeStruct((n, n_col_tiles * tn), dt) for dt in out_defs]
    if small_t is not None:
        out_specs.append(pl.BlockSpec((GATE_ROWS, tm), lambda i, j: (0, i)))
        out_shape.append(jax.ShapeDtypeStruct((GATE_ROWS, n), F32))
    body = functools.partial(_inproj_body, n_parts=n_parts, n_out=len(out_defs), epi=epi,
                             with_small_t=small_t is not None, row_chunk=min(32, tm))
    return pl.pallas_call(
        body,
        grid=(n // tm, n_col_tiles),
        in_specs=in_specs,
        out_specs=out_specs,
        out_shape=out_shape,
        scratch_shapes=[pltpu.VMEM((tm, d), BF16)],
        compiler_params=_params("arbitrary", "arbitrary"),
        name=name,
    )(*args)


def _outproj_body(*refs, row_chunk, with_next):
    if with_next:
        a_ref, w_ref, x_ref, gate_ref, gpost_ref, nshift_ref, nscale_ref, ngpre_ref, o_ref, hn_ref = refs
    else:
        a_ref, w_ref, x_ref, gate_ref, gpost_ref, o_ref = refs
    o_ref[...] = _dot(a_ref[...], w_ref[...])
    tm = o_ref.shape[0]

    gain = gpost_ref[...] * gate_ref[0]
    if with_next:
        next_gain = ngpre_ref[...] * (1.0 + nscale_ref[0])
        next_shift = nshift_ref[0]

    for r in range(tm // row_chunk):
        rows = slice(r * row_chunk, (r + 1) * row_chunk)
        y = o_ref[rows, :]
        xn = x_ref[rows, :] + y * lax.rsqrt(jnp.mean(y * y, axis=-1, keepdims=True) + RMS_EPS) * gain
        o_ref[rows, :] = xn
        if with_next:
            h = xn * lax.rsqrt(jnp.mean(xn * xn, axis=-1, keepdims=True) + RMS_EPS) * next_gain + next_shift
            hn_ref[rows, :] = h.astype(BF16)


def _outproj(a, w, layer, x2, mod3, g_post, *, tm, seq_len, next_pre=None, name="outproj"):
    n, kdim = a.shape
    d = w.shape[2]
    tiles_per_seq = seq_len // tm
    with_next = next_pre is not None
    body = functools.partial(_outproj_body, row_chunk=min(32, tm), with_next=with_next)
    in_specs = [pl.BlockSpec((tm, kdim), lambda i: (i, 0)),
                pl.BlockSpec((None, kdim, d), lambda i: (layer, 0, 0), pipeline_mode=pl.Buffered(1)),
                pl.BlockSpec((tm, d), lambda i: (i, 0)),
                pl.BlockSpec((1, 1, d), lambda i: (i // tiles_per_seq, 0, 2)),
                pl.BlockSpec((1, d), lambda i: (0, 0))]
    args = [a, w, x2, mod3, g_post.reshape(1, d)]
    out_specs = [pl.BlockSpec((tm, d), lambda i: (i, 0))]
    out_shape = [jax.ShapeDtypeStruct((n, d), F32)]
    if with_next:
        next_mod3, next_g_pre = next_pre
        in_specs += [pl.BlockSpec((1, 1, d), lambda i: (i // tiles_per_seq, 0, 0)),
                     pl.BlockSpec((1, 1, d), lambda i: (i // tiles_per_seq, 0, 1)),
                     pl.BlockSpec((1, d), lambda i: (0, 0))]
        args += [next_mod3, next_mod3, next_g_pre.reshape(1, d)]
        out_specs.append(pl.BlockSpec((tm, d), lambda i: (i, 0)))
        out_shape.append(jax.ShapeDtypeStruct((n, d), BF16))
    outs = pl.pallas_call(
        body,
        grid=(n // tm,),
        in_specs=in_specs,
        out_specs=out_specs,
        out_shape=out_shape,
        compiler_params=_params("arbitrary"),
        name=name,
    )(*args)
    return outs if with_next else outs[0]


def _ffn_in_body(hn_ref, wg_ref, wu_ref, o_ref):
    hn = hn_ref[...]
    gate = _dot(hn, wg_ref[...].astype(BF16))
    up = _dot(hn, wu_ref[...].astype(BF16))
    o_ref[...] = (_silu(gate) * up).astype(o_ref.dtype)


def _ffn_in(hn, w, layer, *, tm, tn):
    n, d = hn.shape
    d_ff = w.shape[2] // 2
    nj = d_ff // tn
    return pl.pallas_call(
        _ffn_in_body,
        grid=(nj, n // tm),
        in_specs=[pl.BlockSpec((tm, d), lambda j, i: (i, 0)),
                  pl.BlockSpec((None, d, tn), lambda j, i: (layer, 0, j)),
                  pl.BlockSpec((None, d, tn), lambda j, i: (layer, 0, nj + j))],
        out_specs=pl.BlockSpec((tm, tn), lambda j, i: (i, j)),
        out_shape=jax.ShapeDtypeStruct((n, d_ff), BF16),
        compiler_params=_params("arbitrary", "arbitrary"),
        name="ffn_in",
    )(hn, w, w)


def _dwconv_body(*refs, width, halo, col_chunk, row_tile, mode):
    if mode == "conformer":
        cur_ref, halo_ref, w_ref, b_ref, lng_ref, lnb_ref, o_ref, win_ref, sh_ref, y_ref = refs
    else:
        cur_ref, halo_ref, w_ref, gatein_ref, o_ref, win_ref, sh_ref, y_ref = refs
    t = pl.program_id(1)
    tt, d = y_ref.shape
    off = halo - (width - 1)
    sh_rows = sh_ref.shape[1]

    win_ref[0:halo, :] = jnp.where(t == 0, 0.0, halo_ref[0])
    win_ref[halo:halo + tt, :] = cur_ref[0]

    def col_body(c, carry):
        cs = pl.ds(pl.multiple_of(c * col_chunk, col_chunk), col_chunk)
        sh_ref[0] = win_ref[:, cs]
        for s in range(1, 8):
            sh_ref[s, 0:sh_rows - 8, :] = win_ref[pl.ds(s, sh_rows - 8), cs]
        for r in range(tt // row_tile):
            acc = jnp.zeros((row_tile, col_chunk), F32)
            for k in range(width):
                phase = (off + k) % 8
                base = off + k - phase
                acc = acc + w_ref[k:k + 1, cs] * sh_ref[phase, pl.ds(r * row_tile + base, row_tile), :]
            y_ref[pl.ds(r * row_tile, row_tile), cs] = acc
        return carry
    lax.fori_loop(0, d // col_chunk, col_body, 0)

    def tail(r, carry):
        rows = pl.ds(pl.multiple_of(r * 16, 16), 16)
        if mode == "conformer":
            u = y_ref[rows, :] + b_ref[...]
            mu = jnp.mean(u, axis=-1, keepdims=True)
            var = jnp.mean(jnp.square(u - mu), axis=-1, keepdims=True)
            un = (u - mu) * lax.rsqrt(var + LN_EPS) * lng_ref[...] + lnb_ref[...]
            o_ref[0, rows, :] = _silu(un).astype(o_ref.dtype)
        else:
            o_ref[0, rows, :] = (gatein_ref[0, rows, :] * y_ref[rows, :]).astype(o_ref.dtype)
        return carry
    lax.fori_loop(0, tt // 16, tail, 0, unroll=4)


def _dwconv(u3, w, *, mode, extra, tt):
    bsz, seq, d = u3.shape
    width = w.shape[0]
    halo = 8 * (-(-(width - 1) // 8))
    w_pad = jnp.pad(w, ((0, halo - width), (0, 0)))
    tiles = tt // halo
    in_specs = [pl.BlockSpec((1, tt, d), lambda b, t: (b, t, 0)),
                pl.BlockSpec((1, halo, d), lambda b, t: (b, jnp.maximum(t * tiles - 1, 0), 0)),
                pl.BlockSpec((halo, d), lambda b, t: (0, 0))]
    args = [u3, u3, w_pad]
    if mode == "conformer":
        for v in extra:
            in_specs.append(pl.BlockSpec((1, d), lambda b, t: (0, 0)))
            args.append(v.reshape(1, d))
    else:
        in_specs.append(pl.BlockSpec((1, tt, d), lambda b, t: (b, t, 0)))
        args.append(extra[0])
    col_chunk = min(256, d)
    body = functools.partial(_dwconv_body, width=width, halo=halo, col_chunk=col_chunk,
                             row_tile=min(64, tt), mode=mode)
    return pl.pallas_call(
        body,
        grid=(bsz, seq // tt),
        in_specs=in_specs,
        out_specs=pl.BlockSpec((1, tt, d), lambda b, t: (b, t, 0)),
        out_shape=jax.ShapeDtypeStruct((bsz, seq, d), BF16),
        scratch_shapes=[pltpu.VMEM((halo + tt, d), F32),
                        pltpu.VMEM((8, halo + tt, col_chunk), F32),
                        pltpu.VMEM((tt, d), F32)],
        compiler_params=_params("arbitrary", "arbitrary"),
        name="dwconv_" + mode,
    )(*args)


def _split3_f32(x):
    hi = x.astype(BF16).astype(F32)
    r1 = x - hi
    mid = r1.astype(BF16).astype(F32)
    lo = (r1 - mid).astype(BF16).astype(F32)
    return hi, mid, lo


def _mlstm_body(q_ref, k_ref, v_ref, o_ref, gt_ref, bcol_ref, ng_ref, out_ref, c_ref, m_ref, *, heads,
                head_group_size):
    c = pl.program_id(1)
    L = q_ref.shape[1]
    dk = q_ref.shape[2] // heads
    dv = v_ref.shape[2] // heads
    rep = dv // L

    @pl.when(c == 0)
    def _():
        c_ref[...] = jnp.zeros_like(c_ref)
        m_ref[...] = jnp.zeros_like(m_ref)

    g = gt_ref[...] + bcol_ref[...]
    g = GATE_SOFTCAP * jnp.tanh(g / GATE_SOFTCAP)
    cum = _cumsum_lanes(_log_sigmoid(g))
    li = g[0:heads]
    b = cum[heads:2 * heads]
    r = li - b
    lane = lax.broadcasted_iota(jnp.int32, (heads, L), 1)
    pm = r
    shift = 1
    while shift < L:
        pm = jnp.maximum(pm, jnp.where(lane >= shift, pltpu.roll(pm, shift, 1), -jnp.inf))
        shift *= 2
    m_prev = m_ref[...]
    u = jnp.maximum(m_prev, pm)
    u_last = jnp.broadcast_to(u[:, L - 1:L], (heads, L))
    b_last = jnp.broadcast_to(b[:, L - 1:L], (heads, L))
    m_ref[...] = b_last + u_last
    decay = jnp.exp(m_prev - u_last)
    lhs_terms = _split3_f32(-u) + _split3_f32(m_prev - u) + _split3_f32(-(b + u)) + _split3_f32(r - u_last)
    lhs_rows = (0, 1, 2, 6, 7, 8, 9, 10, 11, 12, 13, 14)
    r_terms = _split3_f32(r)

    rid = lax.broadcasted_iota(jnp.int32, (GATE_ROWS, L), 0)
    lhs_const = jnp.where((rid >= 3) & (rid < 6), 1.0, 0.0)
    rhs_blocks = [jnp.where((rid >= lo) & (rid < lo + 3), 1.0, 0.0).astype(BF16) for lo in (6, 9, 12)]
    rhs0_const = jnp.where(rid < 3, 1.0, 0.0)

    row = lax.broadcasted_iota(jnp.int32, (L, L), 0)
    col = lax.broadcasted_iota(jnp.int32, (L, L), 1)
    causal = col <= row
    ones_ll = jnp.ones((L, L), BF16)
    ones_dv = jnp.ones((dv, L), BF16)

    def operands(h):
        lhs = lhs_const
        for i, term in zip(lhs_rows, lhs_terms):
            lhs = jnp.where(rid == i, term[h:h + 1, :], lhs)
        rhs0 = rhs0_const
        for i, term in zip((3, 4, 5), r_terms):
            rhs0 = jnp.where(rid == i, term[h:h + 1, :], rhs0)
        return lhs.astype(BF16), jnp.concatenate([rhs0.astype(BF16)] + rhs_blocks, axis=1)

    def mean_sq(x):
        xx = x * x
        hi = xx.astype(BF16)
        lo = (xx - hi.astype(F32)).astype(BF16)
        return (_dot(hi, ones_dv) + _dot(lo, ones_dv)) * (1.0 / dv)

    def head_group(H):
        qs = {h: q_ref[0, :, h * dk:(h + 1) * dk] for h in H}
        ks = {h: k_ref[0, :, h * dk:(h + 1) * dk] for h in H}
        vs = {h: v_ref[0, :, h * dv:(h + 1) * dv] for h in H}
        caugs = {h: c_ref[h] for h in H}
        zs = {h: _dot_tn(*operands(h)) for h in H}
        qks = {h: _dot_nt(qs[h], ks[h]) for h in H}
        qcs = {h: _dot(qs[h], caugs[h].astype(BF16)) for h in H}
        ps = {h: jnp.exp(jnp.where(causal, zs[h][:, 0:L], -jnp.inf)) for h in H}
        ss = {h: (qks[h] * ps[h]).astype(BF16) for h in H}
        svs = {h: _dot(ss[h], jnp.concatenate([vs[h], ones_ll], axis=1)) for h in H}
        nds = {h: svs[h] + jnp.concatenate([jnp.exp(zs[h][:, L:2 * L])] * (rep + 1), axis=1) * qcs[h]
               for h in H}
        scales = {h: 1.0 / jnp.maximum(jnp.abs(nds[h][:, dv:]), jnp.exp(zs[h][:, 2 * L:3 * L])) for h in H}
        houts = {h: nds[h][:, :dv] * jnp.concatenate([scales[h]] * rep, axis=1) for h in H}

        for h in H:
            wk = jnp.exp(zs[h][:, 3 * L:4 * L])
            vw = jnp.concatenate([vs[h].astype(F32) * jnp.concatenate([wk] * rep, axis=1), wk], axis=1)
            dec = jnp.concatenate([jnp.broadcast_to(decay[h:h + 1, :], (dk, L))] * (rep + 1), axis=1)
            c_ref[h] = dec * caugs[h] + _dot_tn(ks[h], vw.astype(BF16))

        mss = {h: mean_sq(houts[h]) for h in H}
        for h in H:
            hn = houts[h] * jnp.concatenate([lax.rsqrt(mss[h] + RMS_EPS)] * rep, axis=1)
            hn = hn * ng_ref[:, h * dv:(h + 1) * dv]
            og = _sigmoid(o_ref[0, :, h * dv:(h + 1) * dv].astype(F32))
            out_ref[0, :, h * dv:(h + 1) * dv] = (og * hn).astype(out_ref.dtype)

    for g0 in range(0, heads, head_group_size):
        head_group(range(g0, min(g0 + head_group_size, heads)))


def _mlstm_rec(qkvo3, gt, b_if, norm_g, *, heads, chunk):
    bsz, seq, _ = qkvo3.shape
    dvt = norm_g.shape[0]
    dkt = dvt // 2
    nc = seq // chunk
    assert (dvt // heads) % chunk == 0 and 2 * heads <= GATE_ROWS
    bcol = jnp.pad(b_if, (0, GATE_ROWS - b_if.shape[0])).reshape(GATE_ROWS, 1)
    body = functools.partial(_mlstm_body, heads=heads, head_group_size=4)
    return pl.pallas_call(
        body,
        grid=(bsz, nc),
        in_specs=[pl.BlockSpec((1, chunk, dkt), lambda b, c: (b, c, 0)),
                  pl.BlockSpec((1, chunk, dkt), lambda b, c: (b, c, 1)),
                  pl.BlockSpec((1, chunk, dvt), lambda b, c: (b, c, 1)),
                  pl.BlockSpec((1, chunk, dvt), lambda b, c: (b, c, 2)),
                  pl.BlockSpec((GATE_ROWS, chunk), lambda b, c: (0, b * nc + c)),
                  pl.BlockSpec((GATE_ROWS, 1), lambda b, c: (0, 0)),
                  pl.BlockSpec((1, dvt), lambda b, c: (0, 0))],
        out_specs=pl.BlockSpec((1, chunk, dvt), lambda b, c: (b, c, 0)),
        out_shape=jax.ShapeDtypeStruct((bsz, seq, dvt), BF16),
        scratch_shapes=[pltpu.VMEM((heads, dkt // heads, dvt // heads + chunk), F32),
                        pltpu.VMEM((heads, chunk), F32)],
        compiler_params=_params("arbitrary", "arbitrary"),
        name="mlstm_rec",
    )(qkvo3, qkvo3, qkvo3, qkvo3, gt, bcol, norm_g.reshape(1, dvt))


def _gla_level_masks(chunk):
    t = np.arange(chunk)[:, None]
    s = np.arange(chunk)[None, :]
    masks = []
    b = chunk // 2
    while b >= 1:
        masks.append((t // (2 * b) == s // (2 * b)) & ((t % (2 * b)) >= b) & ((s % (2 * b)) < b))
        b //= 2
    return jnp.asarray(np.stack(masks), dtype=F32)


def _mid_rows(bc, b, row):
    L, dk = bc.shape
    if b >= 4:
        return jnp.concatenate([jnp.broadcast_to(bc[m:m + 1, :], (2 * b, dk)) for m in range(b, L, 2 * b)],
                               axis=0)
    up1 = pltpu.roll(bc, L - 1, 0)
    if b == 1:
        return jnp.where((row & 1) == 0, up1, bc)
    up2 = pltpu.roll(bc, L - 2, 0)
    down1 = pltpu.roll(bc, 1, 0)
    r4 = row & 3
    return jnp.where(r4 == 0, up2, jnp.where(r4 == 1, up1, jnp.where(r4 == 2, bc, down1)))


def _gla_body(q_ref, k_ref, v_ref, r_ref, al_ref, wa_ref, ba_ref, br_ref, ng_ref, mask_ref,
              out_ref, st_ref, *, heads):
    c = pl.program_id(1)
    L = q_ref.shape[1]
    dk = q_ref.shape[2] // heads
    dv = v_ref.shape[2] // heads
    n_levels = mask_ref.shape[0]

    @pl.when(c == 0)
    def _():
        st_ref[...] = jnp.zeros_like(st_ref)

    z = _dot_tn(al_ref[...].astype(BF16), wa_ref[...]) + ba_ref[...]
    log_alpha = _log_sigmoid(z) / GLA_TEMP
    bc_all = _cumsum_rows(log_alpha)

    eye = (lax.broadcasted_iota(jnp.int32, (L, L), 0) == lax.broadcasted_iota(jnp.int32, (L, L), 1))
    row = lax.broadcasted_iota(jnp.int32, (L, dk), 0)

    H = range(heads)
    qfs = [q_ref[0, :, h * dk:(h + 1) * dk].astype(F32) for h in H]
    kfs = [k_ref[0, :, h * dk:(h + 1) * dk].astype(F32) for h in H]
    vs = [v_ref[0, :, h * dv:(h + 1) * dv] for h in H]
    bcs = [bc_all[:, h * dk:(h + 1) * dk] for h in H]
    sts = [st_ref[h] for h in H]

    os_ = [_dot_nt((qfs[h] * jnp.exp(bcs[h])).astype(BF16), sts[h].astype(BF16)) for h in H]

    accs = [jnp.where(eye, jnp.sum(qfs[h] * kfs[h], axis=-1, keepdims=True), 0.0) for h in H]
    for i in range(n_levels):
        b = L >> (i + 1)
        upper = (row & (2 * b - 1)) >= b
        cls = [(jnp.where(upper, qfs[h], kfs[h])
                * jnp.exp(-jnp.abs(bcs[h] - _mid_rows(bcs[h], b, row)))).astype(BF16) for h in H]
        accs = [accs[h] + mask_ref[i] * _dot_nt(cls[h], cls[h]) for h in H]
    os_ = [os_[h] + _dot(accs[h].astype(BF16), vs[h]) for h in H]

    for h in H:
        b_last = bcs[h][L - 1:L, :]
        k_up = (kfs[h] * jnp.exp(jnp.minimum(b_last - bcs[h], 0.0))).astype(BF16)
        st_ref[h] = sts[h] * jnp.exp(b_last) + _dot_tn(vs[h], k_up)

    for h in H:
        o = os_[h]
        on = o * lax.rsqrt(jnp.mean(o * o, axis=-1, keepdims=True) + RMS_EPS)
        on = on * ng_ref[:, h * dv:(h + 1) * dv]
        rg = r_ref[0, :, h * dv:(h + 1) * dv].astype(F32) + br_ref[:, h * dv:(h + 1) * dv]
        out_ref[0, :, h * dv:(h + 1) * dv] = (on * _silu(rg)).astype(out_ref.dtype)


def _gla_rec(qkvr3, al_t, w_alpha, b_alpha, b_r, norm_g, *, heads, chunk):
    bsz, seq, _ = qkvr3.shape
    dvt = norm_g.shape[0]
    dkt = dvt // 2
    nc = seq // chunk
    rank = w_alpha.shape[0]
    assert rank <= GATE_ROWS and chunk % LANE == 0
    wa = jnp.pad(w_alpha, ((0, GATE_ROWS - rank), (0, 0))).astype(BF16)
    masks = _gla_level_masks(chunk)
    body = functools.partial(_gla_body, heads=heads)
    return pl.pallas_call(
        body,
        grid=(bsz, seq // chunk),
        in_specs=[pl.BlockSpec((1, chunk, dkt), lambda b, c: (b, c, 0)),
                  pl.BlockSpec((1, chunk, dkt), lambda b, c: (b, c, 1)),
                  pl.BlockSpec((1, chunk, dvt), lambda b, c: (b, c, 1)),
                  pl.BlockSpec((1, chunk, dvt), lambda b, c: (b, c, 2)),
                  pl.BlockSpec((GATE_ROWS, chunk), lambda b, c: (0, b * nc + c)),
                  pl.BlockSpec((GATE_ROWS, dkt), lambda b, c: (0, 0)),
                  pl.BlockSpec((1, dkt), lambda b, c: (0, 0)),
                  pl.BlockSpec((1, dvt), lambda b, c: (0, 0)),
                  pl.BlockSpec((1, dvt), lambda b, c: (0, 0)),
                  pl.BlockSpec(masks.shape, lambda b, c: (0, 0, 0))],
        out_specs=pl.BlockSpec((1, chunk, dvt), lambda b, c: (b, c, 0)),
        out_shape=jax.ShapeDtypeStruct((bsz, seq, dvt), BF16),
        scratch_shapes=[pltpu.VMEM((heads, dvt // heads, dkt // heads), F32)],
        compiler_params=_params("arbitrary", "arbitrary"),
        name="gla_rec",
    )(qkvr3, qkvr3, qkvr3, qkvr3, al_t, wa, b_alpha.reshape(1, dkt), b_r.reshape(1, dvt),
      norm_g.reshape(1, dvt), masks)


def _epi_glu(accs, j):
    a, g = accs
    return [a * _sigmoid(g)]


def _epi_sconv(accs, j):
    bg, cg, xv = accs
    return [bg, cg * xv]


def _epi_scale_first(accs, j, *, n_first, scale):
    (a,) = accs
    return [a * jnp.where(j < n_first, scale, 1.0)]


def _pad_cols(w, cols):
    return jnp.pad(w, ((0, 0), (0, cols - w.shape[1])))


def kernel(x, c, conf_w_in, conf_w_dw, conf_b_dw, conf_ln_g, conf_ln_b, conf_w_out, sconv_w_in, sconv_w_conv, sconv_w_out, mlstm_w_in, mlstm_b_if, mlstm_norm_g, mlstm_w_out, gla_w_in, gla_w_alpha, gla_b_alpha, gla_b_r, gla_norm_g, gla_w_out, ada_w, ada_b, norm_pre, norm_post, ffn_w_gate_up, ffn_w_down):
    bsz, seq, d = x.shape
    n = bsz * seq
    depth = ada_w.shape[0]
    d_ff = ffn_w_down.shape[1]
    tm_in = _tile(1024, seq)
    tm_out = _tile(512, seq)
    tt = _tile(512, seq)

    mods = _ada_mods(c, ada_w, ada_b)
    x2 = x.reshape(n, d)
    w_down = ffn_w_down.astype(BF16)

    for i in range(depth):
        kind, jj = i % 4, i // 4
        mod3 = mods[2 * i].reshape(bsz, 1, 3 * d)
        g_pre, g_post = norm_pre[i, 0], norm_post[i, 0]
        if kind == 0:
            tn = _tile(512, d)
            (u,) = _inproj(x2, mod3, g_pre, conf_w_in, jj, [0, d // tn], d // tn, _epi_glu,
                           [F32], tm=tm_in, tn=tn, seq_len=seq, name="conf_in")
            a = _dwconv(u.reshape(bsz, seq, d), conf_w_dw[jj], mode="conformer",
                        extra=(conf_b_dw[jj], conf_ln_g[jj], conf_ln_b[jj]), tt=tt)
            w_out = conf_w_out
        elif kind == 1:
            tn = _tile(512, d)
            bg, p = _inproj(x2, mod3, g_pre, sconv_w_in.astype(BF16), jj, [0, d // tn, 2 * d // tn],
                            d // tn, _epi_sconv, [F32, F32], tm=tm_in, tn=tn, seq_len=seq, name="sconv_in")
            a = _dwconv(p.reshape(bsz, seq, d), sconv_w_conv[jj], mode="sconv",
                        extra=(bg.reshape(bsz, seq, d),), tt=tt)
            w_out = sconv_w_out
        elif kind == 2:
            heads = MLSTM_HEADS
            qk, vd = d // 2, d
            w_main = mlstm_w_in[:, :, :2 * qk + 2 * vd].astype(BF16)
            w_if = mlstm_w_in[jj, :, 2 * qk + 2 * vd:]
            tn = _tile(1024, qk)
            epi = functools.partial(_epi_scale_first, n_first=qk // tn, scale=float((qk // heads) ** -0.5))
            qkvo, gt = _inproj(x2, mod3, g_pre, w_main, jj, [0], (2 * qk + 2 * vd) // tn, epi,
                               [BF16], tm=tm_in, tn=tn, seq_len=seq,
                               small_t=_pad_cols(w_if, GATE_ROWS).T, name="mlstm_in")
            a = _mlstm_rec(qkvo.reshape(bsz, seq, -1), gt, mlstm_b_if[jj], mlstm_norm_g[jj], heads=heads,
                           chunk=MLSTM_CHUNK)
            w_out = mlstm_w_out
        else:
            heads = GLA_HEADS
            kd, vd = d // 2, d
            w_main = gla_w_in[:, :, :2 * kd + 2 * vd].astype(BF16)
            w_al = gla_w_in[jj, :, 2 * kd + 2 * vd:]
            tn = _tile(1024, kd)
            epi = functools.partial(_epi_scale_first, n_first=kd // tn, scale=float((kd // heads) ** -0.5))
            qkvr, al_t = _inproj(x2, mod3, g_pre, w_main, jj, [0], (2 * kd + 2 * vd) // tn, epi,
                                 [BF16], tm=tm_in, tn=tn, seq_len=seq,
                                 small_t=_pad_cols(w_al, GATE_ROWS).T, name="gla_in")
            a = _gla_rec(qkvr.reshape(bsz, seq, -1), al_t, gla_w_alpha[jj],
                         gla_b_alpha[jj], gla_b_r[jj], gla_norm_g[jj], heads=heads, chunk=GLA_CHUNK)
            w_out = gla_w_out
        ffn_mod3 = mods[2 * i + 1].reshape(bsz, 1, 3 * d)
        x2, hn = _outproj(a.reshape(n, d), w_out.astype(BF16), jj, x2, mod3, g_post, tm=tm_out,
                          seq_len=seq, next_pre=(ffn_mod3, norm_pre[i, 1]), name="mixer_out")
        act = _ffn_in(hn, ffn_w_gate_up, i, tm=_tile(2048, n), tn=_tile(512, d_ff))
        x2 = _outproj(act, w_down, i, x2, ffn_mod3, norm_post[i, 1], tm=tm_out, seq_len=seq,
                      name="ffn_out")

    return x2.reshape(bsz, seq, d)
```

```python
import functools

import numpy as np
import jax
import jax.numpy as jnp
from jax import lax
from jax.experimental import pallas as pl
from jax.experimental.pallas import tpu as pltpu

F32 = jnp.float32
BF16 = jnp.bfloat16

RMS_EPS = 1e-6
LN_EPS = 1e-5
MLSTM_HEADS = 8
MLSTM_CHUNK = 256
GATE_SOFTCAP = 15.0
GLA_HEADS = 4
GLA_CHUNK = 256
GLA_TEMP = 16.0

LANE = 128
GATE_ROWS = 16
VMEM_LIMIT_BYTES = 56 * 1024 * 1024


def _tile(pref, dim):
    return pref if dim % pref == 0 else dim


def _params(*sem):
    return pltpu.CompilerParams(dimension_semantics=sem, vmem_limit_bytes=VMEM_LIMIT_BYTES)


def _sigmoid(z):
    return 1.0 / (1.0 + jnp.exp(-z))


def _silu(z):
    return z * _sigmoid(z)


def _log_sigmoid(z):
    return jnp.minimum(z, 0.0) - jnp.log(1.0 + jnp.exp(-jnp.abs(z)))


def _dot(a, b):
    return jnp.dot(a, b, preferred_element_type=F32)


def _dot_nt(a, b):
    return lax.dot_general(a, b, (((1,), (1,)), ((), ())), preferred_element_type=F32)


def _dot_tn(a, b):
    return lax.dot_general(a, b, (((0,), (0,)), ((), ())), preferred_element_type=F32)


def _split3(x):
    hi = x.astype(BF16)
    r1 = x - hi.astype(F32)
    mid = r1.astype(BF16)
    lo = (r1 - mid.astype(F32)).astype(BF16)
    return hi, mid, lo


def _tri(n, upper):
    r = lax.broadcasted_iota(jnp.int32, (n, n), 0)
    c = lax.broadcasted_iota(jnp.int32, (n, n), 1)
    return jnp.where((r <= c) if upper else (c <= r), 1.0, 0.0).astype(BF16)


def _cumsum_rows(x):
    tri = _tri(x.shape[0], upper=False)
    hi, mid, lo = _split3(x)
    return _dot(tri, hi) + _dot(tri, mid) + _dot(tri, lo)


def _cumsum_lanes(x):
    tri = _tri(x.shape[1], upper=True)
    hi, mid, lo = _split3(x)
    return _dot(hi, tri) + _dot(mid, tri) + _dot(lo, tri)


def _ada_body(c_ref, w_ref, b_ref, o_ref):
    sc = _silu(c_ref[...]).astype(BF16)
    o_ref[0] = _dot(sc, w_ref[0].astype(BF16)) + b_ref[0]


def _ada_mods(c, ada_w, ada_b):
    depth, two, d, d3 = ada_w.shape
    n_sub = depth * two
    bsz = c.shape[0]
    rows = -(-bsz // 16) * 16
    c_pad = jnp.pad(c, ((0, rows - bsz), (0, 0)))
    tn = _tile(1024, d3)
    out = pl.pallas_call(
        _ada_body,
        grid=(n_sub, d3 // tn),
        in_specs=[pl.BlockSpec((rows, d), lambda l, j: (0, 0)),
                  pl.BlockSpec((1, d, tn), lambda l, j: (l, 0, j)),
                  pl.BlockSpec((1, 1, tn), lambda l, j: (l, 0, j))],
        out_specs=pl.BlockSpec((1, rows, tn), lambda l, j: (l, 0, j)),
        out_shape=jax.ShapeDtypeStruct((n_sub, rows, d3), F32),
        compiler_params=_params("arbitrary", "arbitrary"),
        name="ada_mods",
    )(c_pad, ada_w.reshape(n_sub, d, d3), ada_b.reshape(n_sub, 1, d3))
    return out[:, :bsz, :]


def _inproj_body(*refs, n_parts, n_out, epi, with_small_t, row_chunk):
    x_ref, shift_ref, scale_ref, gpre_ref = refs[:4]
    w_refs = refs[4:4 + n_parts]
    pos = 4 + n_parts
    wst_ref = gt_ref = None
    if with_small_t:
        wst_ref = refs[pos]
        pos += 1
    out_refs = refs[pos:pos + n_out]
    pos += n_out
    if with_small_t:
        gt_ref = refs[pos]
        pos += 1
    hn_ref = refs[pos]

    j = pl.program_id(1)
    tm = x_ref.shape[0]

    @pl.when(j == 0)
    def _():
        gain = gpre_ref[...] * (1.0 + scale_ref[0])
        shift = shift_ref[0]

        def chunk(r, carry):
            rows = pl.ds(pl.multiple_of(r * row_chunk, row_chunk), row_chunk)
            x = x_ref[rows, :]
            h = x * lax.rsqrt(jnp.mean(x * x, axis=-1, keepdims=True) + RMS_EPS) * gain + shift
            hn_ref[rows, :] = h.astype(BF16)
            return carry
        lax.fori_loop(0, tm // row_chunk, chunk, 0, unroll=4)
        if with_small_t:
            gt_ref[...] = _dot_nt(wst_ref[...].astype(BF16), hn_ref[...])

    hn = hn_ref[...]
    accs = [_dot(hn, w_ref[...].astype(BF16)) for w_ref in w_refs]
    outs = epi(accs, j)
    for o_ref, o in zip(out_refs, outs):
        o_ref[...] = o.astype(o_ref.dtype)


def _inproj(x2, mod3, g_pre, w, layer, part_offsets, n_col_tiles, epi, out_defs, *, tm, tn, seq_len,
            small_t=None, name="inproj"):
    n, d = x2.shape
    tiles_per_seq = seq_len // tm
    n_parts = len(part_offsets)
    in_specs = [pl.BlockSpec((tm, d), lambda i, j: (i, 0)),
                pl.BlockSpec((1, 1, d), lambda i, j: (i // tiles_per_seq, 0, 0)),
                pl.BlockSpec((1, 1, d), lambda i, j: (i // tiles_per_seq, 0, 1)),
                pl.BlockSpec((1, d), lambda i, j: (0, 0))]
    args = [x2, mod3, mod3, g_pre.reshape(1, d)]
    for off in part_offsets:
        in_specs.append(pl.BlockSpec((None, d, tn), lambda i, j, off=off: (layer, 0, off + j)))
        args.append(w)
    if small_t is not None:
        in_specs.append(pl.BlockSpec((GATE_ROWS, d), lambda i, j: (0, 0)))
        args.append(small_t)
    out_specs = [pl.BlockSpec((tm, tn), lambda i, j: (i, j)) for _ in out_defs]
    out_shape = [jax.ShapeDtypeStruct((n, n_col_tiles * tn), dt) for dt in out_defs]
    if small_t is not None:
        out_specs.append(pl.BlockSpec((GATE_ROWS, tm), lambda i, j: (0, i)))
        out_shape.append(jax.ShapeDtypeStruct((GATE_ROWS, n), F32))
    body = functools.partial(_inproj_body, n_parts=n_parts, n_out=len(out_defs), epi=epi,
                             with_small_t=small_t is not None, row_chunk=min(32, tm))
    return pl.pallas_call(
        body,
        grid=(n // tm, n_col_tiles),
        in_specs=in_specs,
        out_specs=out_specs,
        out_shape=out_shape,
        scratch_shapes=[pltpu.VMEM((tm, d), BF16)],
        compiler_params=_params("arbitrary", "arbitrary"),
        name=name,
    )(*args)


def _outproj_body(*refs, row_chunk, with_next):
    if with_next:
        a_ref, w_ref, x_ref, gate_ref, gpost_ref, nshift_ref, nscale_ref, ngpre_ref, o_ref, hn_ref = refs
    else:
        a_ref, w_ref, x_ref, gate_ref, gpost_ref, o_ref = refs
    o_ref[...] = _dot(a_ref[...], w_ref[...])
    tm = o_ref.shape[0]

    gain = gpost_ref[...] * gate_ref[0]
    if with_next:
        next_gain = ngpre_ref[...] * (1.0 + nscale_ref[0])
        next_shift = nshift_ref[0]

    for r in range(tm // row_chunk):
        rows = slice(r * row_chunk, (r + 1) * row_chunk)
        y = o_ref[rows, :]
        xn = x_ref[rows, :] + y * lax.rsqrt(jnp.mean(y * y, axis=-1, keepdims=True) + RMS_EPS) * gain
        o_ref[rows, :] = xn
        if with_next:
            h = xn * lax.rsqrt(jnp.mean(xn * xn, axis=-1, keepdims=True) + RMS_EPS) * next_gain + next_shift
            hn_ref[rows, :] = h.astype(BF16)


def _outproj(a, w, layer, x2, mod3, g_post, *, tm, seq_len, next_pre=None, name="outproj"):
    n, kdim = a.shape
    d = w.shape[2]
    tiles_per_seq = seq_len // tm
    with_next = next_pre is not None
    body = functools.partial(_outproj_body, row_chunk=min(32, tm), with_next=with_next)
    in_specs = [pl.BlockSpec((tm, kdim), lambda i: (i, 0)),
                pl.BlockSpec((None, kdim, d), lambda i: (layer, 0, 0), pipeline_mode=pl.Buffered(1)),
                pl.BlockSpec((tm, d), lambda i: (i, 0)),
                pl.BlockSpec((1, 1, d), lambda i: (i // tiles_per_seq, 0, 2)),
                pl.BlockSpec((1, d), lambda i: (0, 0))]
    args = [a, w, x2, mod3, g_post.reshape(1, d)]
    out_specs = [pl.BlockSpec((tm, d), lambda i: (i, 0))]
    out_shape = [jax.ShapeDtypeStruct((n, d), F32)]
    if with_next:
        next_mod3, next_g_pre = next_pre
        in_specs += [pl.BlockSpec((1, 1, d), lambda i: (i // tiles_per_seq, 0, 0)),
                     pl.BlockSpec((1, 1, d), lambda i: (i // tiles_per_seq, 0, 1)),
                     pl.BlockSpec((1, d), lambda i: (0, 0))]
        args += [next_mod3, next_mod3, next_g_pre.reshape(1, d)]
        out_specs.append(pl.BlockSpec((tm, d), lambda i: (i, 0)))
        out_shape.append(jax.ShapeDtypeStruct((n, d), BF16))
    outs = pl.pallas_call(
        body,
        grid=(n // tm,),
        in_specs=in_specs,
        out_specs=out_specs,
        out_shape=out_shape,
        compiler_params=_params("arbitrary"),
        name=name,
    )(*args)
    return outs if with_next else outs[0]


def _ffn_in_body(hn_ref, wg_ref, wu_ref, o_ref):
    hn = hn_ref[...]
    gate = _dot(hn, wg_ref[...].astype(BF16))
    up = _dot(hn, wu_ref[...].astype(BF16))
    o_ref[...] = (_silu(gate) * up).astype(o_ref.dtype)


def _ffn_in(hn, w, layer, *, tm, tn):
    n, d = hn.shape
    d_ff = w.shape[2] // 2
    nj = d_ff // tn
    return pl.pallas_call(
        _ffn_in_body,
        grid=(nj, n // tm),
        in_specs=[pl.BlockSpec((tm, d), lambda j, i: (i, 0)),
                  pl.BlockSpec((None, d, tn), lambda j, i: (layer, 0, j)),
                  pl.BlockSpec((None, d, tn), lambda j, i: (layer, 0, nj + j))],
        out_specs=pl.BlockSpec((tm, tn), lambda j, i: (i, j)),
        out_shape=jax.ShapeDtypeStruct((n, d_ff), BF16),
        compiler_params=_params("arbitrary", "arbitrary"),
        name="ffn_in",
    )(hn, w, w)


def _dwconv_body(*refs, width, halo, col_chunk, row_tile, mode):
    if mode == "conformer":
        cur_ref, halo_ref, w_ref, b_ref, lng_ref, lnb_ref, o_ref, win_ref, sh_ref, y_ref = refs
    else:
        cur_ref, halo_ref, w_ref, gatein_ref, o_ref, win_ref, sh_ref, y_ref = refs
    t = pl.program_id(1)
    tt, d = y_ref.shape
    off = halo - (width - 1)
    sh_rows = sh_ref.shape[1]

    win_ref[0:halo, :] = jnp.where(t == 0, 0.0, halo_ref[0])
    win_ref[halo:halo + tt, :] = cur_ref[0]

    def col_body(c, carry):
        cs = pl.ds(pl.multiple_of(c * col_chunk, col_chunk), col_chunk)
        sh_ref[0] = win_ref[:, cs]
        for s in range(1, 8):
            sh_ref[s, 0:sh_rows - 8, :] = win_ref[pl.ds(s, sh_rows - 8), cs]
        for r in range(tt // row_tile):
            acc = jnp.zeros((row_tile, col_chunk), F32)
            for k in range(width):
                phase = (off + k) % 8
                base = off + k - phase
                acc = acc + w_ref[k:k + 1, cs] * sh_ref[phase, pl.ds(r * row_tile + base, row_tile), :]
            y_ref[pl.ds(r * row_tile, row_tile), cs] = acc
        return carry
    lax.fori_loop(0, d // col_chunk, col_body, 0)

    def tail(r, carry):
        rows = pl.ds(pl.multiple_of(r * 16, 16), 16)
        if mode == "conformer":
            u = y_ref[rows, :] + b_ref[...]
            mu = jnp.mean(u, axis=-1, keepdims=True)
            var = jnp.mean(jnp.square(u - mu), axis=-1, keepdims=True)
            un = (u - mu) * lax.rsqrt(var + LN_EPS) * lng_ref[...] + lnb_ref[...]
            o_ref[0, rows, :] = _silu(un).astype(o_ref.dtype)
        else:
            o_ref[0, rows, :] = (gatein_ref[0, rows, :] * y_ref[rows, :]).astype(o_ref.dtype)
        return carry
    lax.fori_loop(0, tt // 16, tail, 0, unroll=4)


def _dwconv(u3, w, *, mode, extra, tt):
    bsz, seq, d = u3.shape
    width = w.shape[0]
    halo = 8 * (-(-(width - 1) // 8))
    w_pad = jnp.pad(w, ((0, halo - width), (0, 0)))
    tiles = tt // halo
    in_specs = [pl.BlockSpec((1, tt, d), lambda b, t: (b, t, 0)),
                pl.BlockSpec((1, halo, d), lambda b, t: (b, jnp.maximum(t * tiles - 1, 0), 0)),
                pl.BlockSpec((halo, d), lambda b, t: (0, 0))]
    args = [u3, u3, w_pad]
    if mode == "conformer":
        for v in extra:
            in_specs.append(pl.BlockSpec((1, d), lambda b, t: (0, 0)))
            args.append(v.reshape(1, d))
    else:
        in_specs.append(pl.BlockSpec((1, tt, d), lambda b, t: (b, t, 0)))
        args.append(extra[0])
    col_chunk = min(256, d)
    body = functools.partial(_dwconv_body, width=width, halo=halo, col_chunk=col_chunk,
                             row_tile=min(64, tt), mode=mode)
    return pl.pallas_call(
        body,
        grid=(bsz, seq // tt),
        in_specs=in_specs,
        out_specs=pl.BlockSpec((1, tt, d), lambda b, t: (b, t, 0)),
        out_shape=jax.ShapeDtypeStruct((bsz, seq, d), BF16),
        scratch_shapes=[pltpu.VMEM((halo + tt, d), F32),
                        pltpu.VMEM((8, halo + tt, col_chunk), F32),
                        pltpu.VMEM((tt, d), F32)],
        compiler_params=_params("arbitrary", "arbitrary"),
        name="dwconv_" + mode,
    )(*args)


def _split3_f32(x):
    hi = x.astype(BF16).astype(F32)
    r1 = x - hi
    mid = r1.astype(BF16).astype(F32)
    lo = (r1 - mid).astype(BF16).astype(F32)
    return hi, mid, lo


def _mlstm_body(q_ref, k_ref, v_ref, o_ref, gt_ref, bcol_ref, ng_ref, out_ref, c_ref, m_ref, *, heads,
                head_group_size):
    c = pl.program_id(1)
    L = q_ref.shape[1]
    dk = q_ref.shape[2] // heads
    dv = v_ref.shape[2] // heads
    rep = dv // L

    @pl.when(c == 0)
    def _():
        c_ref[...] = jnp.zeros_like(c_ref)
        m_ref[...] = jnp.zeros_like(m_ref)

    g = gt_ref[...] + bcol_ref[...]
    g = GATE_SOFTCAP * jnp.tanh(g / GATE_SOFTCAP)
    cum = _cumsum_lanes(_log_sigmoid(g))
    li = g[0:heads]
    b = cum[heads:2 * heads]
    r = li - b
    lane = lax.broadcasted_iota(jnp.int32, (heads, L), 1)
    pm = r
    shift = 1
    while shift < L:
        pm = jnp.maximum(pm, jnp.where(lane >= shift, pltpu.roll(pm, shift, 1), -jnp.inf))
        shift *= 2
    m_prev = m_ref[...]
    u = jnp.maximum(m_prev, pm)
    u_last = jnp.broadcast_to(u[:, L - 1:L], (heads, L))
    b_last = jnp.broadcast_to(b[:, L - 1:L], (heads, L))
    m_ref[...] = b_last + u_last
    decay = jnp.exp(m_prev - u_last)
    lhs_terms = _split3_f32(-u) + _split3_f32(m_prev - u) + _split3_f32(-(b + u)) + _split3_f32(r - u_last)
    lhs_rows = (0, 1, 2, 6, 7, 8, 9, 10, 11, 12, 13, 14)
    r_terms = _split3_f32(r)

    rid = lax.broadcasted_iota(jnp.int32, (GATE_ROWS, L), 0)
    lhs_const = jnp.where((rid >= 3) & (rid < 6), 1.0, 0.0)
    rhs_blocks = [jnp.where((rid >= lo) & (rid < lo + 3), 1.0, 0.0).astype(BF16) for lo in (6, 9, 12)]
    rhs0_const = jnp.where(rid < 3, 1.0, 0.0)

    row = lax.broadcasted_iota(jnp.int32, (L, L), 0)
    col = lax.broadcasted_iota(jnp.int32, (L, L), 1)
    causal = col <= row
    ones_ll = jnp.ones((L, L), BF16)
    ones_dv = jnp.ones((dv, L), BF16)

    def operands(h):
        lhs = lhs_const
        for i, term in zip(lhs_rows, lhs_terms):
            lhs = jnp.where(rid == i, term[h:h + 1, :], lhs)
        rhs0 = rhs0_const
        for i, term in zip((3, 4, 5), r_terms):
            rhs0 = jnp.where(rid == i, term[h:h + 1, :], rhs0)
        return lhs.astype(BF16), jnp.concatenate([rhs0.astype(BF16)] + rhs_blocks, axis=1)

    def mean_sq(x):
        xx = x * x
        hi = xx.astype(BF16)
        lo = (xx - hi.astype(F32)).astype(BF16)
        return (_dot(hi, ones_dv) + _dot(lo, ones_dv)) * (1.0 / dv)

    def head_group(H):
        qs = {h: q_ref[0, :, h * dk:(h + 1) * dk] for h in H}
        ks = {h: k_ref[0, :, h * dk:(h + 1) * dk] for h in H}
        vs = {h: v_ref[0, :, h * dv:(h + 1) * dv] for h in H}
        caugs = {h: c_ref[h] for h in H}
        zs = {h: _dot_tn(*operands(h)) for h in H}
        qks = {h: _dot_nt(qs[h], ks[h]) for h in H}
        qcs = {h: _dot(qs[h], caugs[h].astype(BF16)) for h in H}
        ps = {h: jnp.exp(jnp.where(causal, zs[h][:, 0:L], -jnp.inf)) for h in H}
        ss = {h: (qks[h] * ps[h]).astype(BF16) for h in H}
        svs = {h: _dot(ss[h], jnp.concatenate([vs[h], ones_ll], axis=1)) for h in H}
        nds = {h: svs[h] + jnp.concatenate([jnp.exp(zs[h][:, L:2 * L])] * (rep + 1), axis=1) * qcs[h]
               for h in H}
        scales = {h: 1.0 / jnp.maximum(jnp.abs(nds[h][:, dv:]), jnp.exp(zs[h][:, 2 * L:3 * L])) for h in H}
        houts = {h: nds[h][:, :dv] * jnp.concatenate([scales[h]] * rep, axis=1) for h in H}

        for h in H:
            wk = jnp.exp(zs[h][:, 3 * L:4 * L])
            vw = jnp.concatenate([vs[h].astype(F32) * jnp.concatenate([wk] * rep, axis=1), wk], axis=1)
            dec = jnp.concatenate([jnp.broadcast_to(decay[h:h + 1, :], (dk, L))] * (rep + 1), axis=1)
            c_ref[h] = dec * caugs[h] + _dot_tn(ks[h], vw.astype(BF16))

        mss = {h: mean_sq(houts[h]) for h in H}
        for h in H:
            hn = houts[h] * jnp.concatenate([lax.rsqrt(mss[h] + RMS_EPS)] * rep, axis=1)
            hn = hn * ng_ref[:, h * dv:(h + 1) * dv]
            og = _sigmoid(o_ref[0, :, h * dv:(h + 1) * dv].astype(F32))
            out_ref[0, :, h * dv:(h + 1) * dv] = (og * hn).astype(out_ref.dtype)

    for g0 in range(0, heads, head_group_size):
        head_group(range(g0, min(g0 + head_group_size, heads)))


def _mlstm_rec(qkvo3, gt, b_if, norm_g, *, heads, chunk):
    bsz, seq, _ = qkvo3.shape
    dvt = norm_g.shape[0]
    dkt = dvt // 2
    nc = seq // chunk
    assert (dvt // heads) % chunk == 0 and 2 * heads <= GATE_ROWS
    bcol = jnp.pad(b_if, (0, GATE_ROWS - b_if.shape[0])).reshape(GATE_ROWS, 1)
    body = functools.partial(_mlstm_body, heads=heads, head_group_size=4)
    return pl.pallas_call(
        body,
        grid=(bsz, nc),
        in_specs=[pl.BlockSpec((1, chunk, dkt), lambda b, c: (b, c, 0)),
                  pl.BlockSpec((1, chunk, dkt), lambda b, c: (b, c, 1)),
                  pl.BlockSpec((1, chunk, dvt), lambda b, c: (b, c, 1)),
                  pl.BlockSpec((1, chunk, dvt), lambda b, c: (b, c, 2)),
                  pl.BlockSpec((GATE_ROWS, chunk), lambda b, c: (0, b * nc + c)),
                  pl.BlockSpec((GATE_ROWS, 1), lambda b, c: (0, 0)),
                  pl.BlockSpec((1, dvt), lambda b, c: (0, 0))],
        out_specs=pl.BlockSpec((1, chunk, dvt), lambda b, c: (b, c, 0)),
        out_shape=jax.ShapeDtypeStruct((bsz, seq, dvt), BF16),
        scratch_shapes=[pltpu.VMEM((heads, dkt // heads, dvt // heads + chunk), F32),
                        pltpu.VMEM((heads, chunk), F32)],
        compiler_params=_params("arbitrary", "arbitrary"),
        name="mlstm_rec",
    )(qkvo3, qkvo3, qkvo3, qkvo3, gt, bcol, norm_g.reshape(1, dvt))


def _gla_level_masks(chunk):
    t = np.arange(chunk)[:, None]
    s = np.arange(chunk)[None, :]
    masks = []
    b = chunk // 2
    while b >= 1:
        masks.append((t // (2 * b) == s // (2 * b)) & ((t % (2 * b)) >= b) & ((s % (2 * b)) < b))
        b //= 2
    return jnp.asarray(np.stack(masks), dtype=F32)


def _mid_rows(bc, b, row):
    L, dk = bc.shape
    if b >= 4:
        return jnp.concatenate([jnp.broadcast_to(bc[m:m + 1, :], (2 * b, dk)) for m in range(b, L, 2 * b)],
                               axis=0)
    up1 = pltpu.roll(bc, L - 1, 0)
    if b == 1:
        return jnp.where((row & 1) == 0, up1, bc)
    up2 = pltpu.roll(bc, L - 2, 0)
    down1 = pltpu.roll(bc, 1, 0)
    r4 = row & 3
    return jnp.where(r4 == 0, up2, jnp.where(r4 == 1, up1, jnp.where(r4 == 2, bc, down1)))


def _gla_body(q_ref, k_ref, v_ref, r_ref, al_ref, wa_ref, ba_ref, br_ref, ng_ref, mask_ref,
              out_ref, st_ref, *, heads, head_group_size):
    c = pl.program_id(1)
    L = q_ref.shape[1]
    dk = q_ref.shape[2] // heads
    dv = v_ref.shape[2] // heads
    n_levels = mask_ref.shape[0]

    @pl.when(c == 0)
    def _():
        st_ref[...] = jnp.zeros_like(st_ref)

    z = _dot_tn(al_ref[...].astype(BF16), wa_ref[...]) + ba_ref[...]
    log_alpha = _log_sigmoid(z) / GLA_TEMP
    bc_all = _cumsum_rows(log_alpha)

    eye = (lax.broadcasted_iota(jnp.int32, (L, L), 0) == lax.broadcasted_iota(jnp.int32, (L, L), 1))
    row = lax.broadcasted_iota(jnp.int32, (L, dk), 0)

    def head_group(H):
        qfs = {h: q_ref[0, :, h * dk:(h + 1) * dk].astype(F32) for h in H}
        kfs = {h: k_ref[0, :, h * dk:(h + 1) * dk].astype(F32) for h in H}
        vs = {h: v_ref[0, :, h * dv:(h + 1) * dv] for h in H}
        bcs = {h: bc_all[:, h * dk:(h + 1) * dk] for h in H}
        sts = {h: st_ref[h] for h in H}

        os_ = {h: _dot_nt((qfs[h] * jnp.exp(bcs[h])).astype(BF16), sts[h].astype(BF16)) for h in H}

        accs = {h: jnp.where(eye, jnp.sum(qfs[h] * kfs[h], axis=-1, keepdims=True), 0.0) for h in H}
        for i in range(n_levels):
            b = L >> (i + 1)
            upper = (row & (2 * b - 1)) >= b
            cls = {h: (jnp.where(upper, qfs[h], kfs[h])
                       * jnp.exp(-jnp.abs(bcs[h] - _mid_rows(bcs[h], b, row)))).astype(BF16) for h in H}
            accs = {h: accs[h] + mask_ref[i] * _dot_nt(cls[h], cls[h]) for h in H}
        os_ = {h: os_[h] + _dot(accs[h].astype(BF16), vs[h]) for h in H}

        for h in H:
            b_last = bcs[h][L - 1:L, :]
            k_up = (kfs[h] * jnp.exp(jnp.minimum(b_last - bcs[h], 0.0))).astype(BF16)
            st_ref[h] = sts[h] * jnp.exp(b_last) + _dot_tn(vs[h], k_up)

        for h in H:
            o = os_[h]
            on = o * lax.rsqrt(jnp.mean(o * o, axis=-1, keepdims=True) + RMS_EPS)
            on = on * ng_ref[:, h * dv:(h + 1) * dv]
            rg = r_ref[0, :, h * dv:(h + 1) * dv].astype(F32) + br_ref[:, h * dv:(h + 1) * dv]
            out_ref[0, :, h * dv:(h + 1) * dv] = (on * _silu(rg)).astype(out_ref.dtype)

    for g0 in range(0, heads, head_group_size):
        head_group(range(g0, min(g0 + head_group_size, heads)))


def _gla_rec(qkvr3, al_t, w_alpha, b_alpha, b_r, norm_g, *, heads, chunk):
    bsz, seq, _ = qkvr3.shape
    dvt = norm_g.shape[0]
    dkt = dvt // 2
    nc = seq // chunk
    rank = w_alpha.shape[0]
    assert rank <= GATE_ROWS and chunk % LANE == 0
    wa = jnp.pad(w_alpha, ((0, GATE_ROWS - rank), (0, 0))).astype(BF16)
    masks = _gla_level_masks(chunk)
    body = functools.partial(_gla_body, heads=heads, head_group_size=4)
    return pl.pallas_call(
        body,
        grid=(bsz, seq // chunk),
        in_specs=[pl.BlockSpec((1, chunk, dkt), lambda b, c: (b, c, 0)),
                  pl.BlockSpec((1, chunk, dkt), lambda b, c: (b, c, 1)),
                  pl.BlockSpec((1, chunk, dvt), lambda b, c: (b, c, 1)),
                  pl.BlockSpec((1, chunk, dvt), lambda b, c: (b, c, 2)),
                  pl.BlockSpec((GATE_ROWS, chunk), lambda b, c: (0, b * nc + c)),
                  pl.BlockSpec((GATE_ROWS, dkt), lambda b, c: (0, 0)),
                  pl.BlockSpec((1, dkt), lambda b, c: (0, 0)),
                  pl.BlockSpec((1, dvt), lambda b, c: (0, 0)),
                  pl.BlockSpec((1, dvt), lambda b, c: (0, 0)),
                  pl.BlockSpec(masks.shape, lambda b, c: (0, 0, 0))],
        out_specs=pl.BlockSpec((1, chunk, dvt), lambda b, c: (b, c, 0)),
        out_shape=jax.ShapeDtypeStruct((bsz, seq, dvt), BF16),
        scratch_shapes=[pltpu.VMEM((heads, dvt // heads, dkt // heads), F32)],
        compiler_params=_params("arbitrary", "arbitrary"),
        name="gla_rec",
    )(qkvr3, qkvr3, qkvr3, qkvr3, al_t, wa, b_alpha.reshape(1, dkt), b_r.reshape(1, dvt),
      norm_g.reshape(1, dvt), masks)


def _epi_glu(accs, j):
    a, g = accs
    return [a * _sigmoid(g)]


def _epi_sconv(accs, j):
    bg, cg, xv = accs
    return [bg, cg * xv]


def _epi_scale_first(accs, j, *, n_first, scale):
    (a,) = accs
    return [a * jnp.where(j < n_first, scale, 1.0)]


def _pad_cols(w, cols):
    return jnp.pad(w, ((0, 0), (0, cols - w.shape[1])))


def kernel(x, c, conf_w_in, conf_w_dw, conf_b_dw, conf_ln_g, conf_ln_b, conf_w_out, sconv_w_in, sconv_w_conv, sconv_w_out, mlstm_w_in, mlstm_b_if, mlstm_norm_g, mlstm_w_out, gla_w_in, gla_w_alpha, gla_b_alpha, gla_b_r, gla_norm_g, gla_w_out, ada_w, ada_b, norm_pre, norm_post, ffn_w_gate_up, ffn_w_down):
    bsz, seq, d = x.shape
    n = bsz * seq
    depth = ada_w.shape[0]
    d_ff = ffn_w_down.shape[1]
    tm_in = _tile(1024, seq)
    tm_out = _tile(512, seq)
    tt = _tile(512, seq)

    mods = _ada_mods(c, ada_w, ada_b)
    x2 = x.reshape(n, d)
    w_down = ffn_w_down.astype(BF16)

    for i in range(depth):
        kind, jj = i % 4, i // 4
        mod3 = mods[2 * i].reshape(bsz, 1, 3 * d)
        g_pre, g_post = norm_pre[i, 0], norm_post[i, 0]
        if kind == 0:
            tn = _tile(512, d)
            (u,) = _inproj(x2, mod3, g_pre, conf_w_in, jj, [0, d // tn], d // tn, _epi_glu,
                           [F32], tm=tm_in, tn=tn, seq_len=seq, name="conf_in")
            a = _dwconv(u.reshape(bsz, seq, d), conf_w_dw[jj], mode="conformer",
                        extra=(conf_b_dw[jj], conf_ln_g[jj], conf_ln_b[jj]), tt=tt)
            w_out = conf_w_out
        elif kind == 1:
            tn = _tile(512, d)
            bg, p = _inproj(x2, mod3, g_pre, sconv_w_in.astype(BF16), jj, [0, d // tn, 2 * d // tn],
                            d // tn, _epi_sconv, [F32, F32], tm=tm_in, tn=tn, seq_len=seq, name="sconv_in")
            a = _dwconv(p.reshape(bsz, seq, d), sconv_w_conv[jj], mode="sconv",
                        extra=(bg.reshape(bsz, seq, d),), tt=tt)
            w_out = sconv_w_out
        elif kind == 2:
            heads = MLSTM_HEADS
            qk, vd = d // 2, d
            w_main = mlstm_w_in[:, :, :2 * qk + 2 * vd].astype(BF16)
            w_if = mlstm_w_in[jj, :, 2 * qk + 2 * vd:]
            tn = _tile(1024, qk)
            epi = functools.partial(_epi_scale_first, n_first=qk // tn, scale=float((qk // heads) ** -0.5))
            qkvo, gt = _inproj(x2, mod3, g_pre, w_main, jj, [0], (2 * qk + 2 * vd) // tn, epi,
                               [BF16], tm=tm_in, tn=tn, seq_len=seq,
                               small_t=_pad_cols(w_if, GATE_ROWS).T, name="mlstm_in")
            a = _mlstm_rec(qkvo.reshape(bsz, seq, -1), gt, mlstm_b_if[jj], mlstm_norm_g[jj], heads=heads,
                           chunk=MLSTM_CHUNK)
            w_out = mlstm_w_out
        else:
            heads = GLA_HEADS
            kd, vd = d // 2, d
            w_main = gla_w_in[:, :, :2 * kd + 2 * vd].astype(BF16)
            w_al = gla_w_in[jj, :, 2 * kd + 2 * vd:]
            tn = _tile(1024, kd)
            epi = functools.partial(_epi_scale_first, n_first=kd // tn, scale=float((kd // heads) ** -0.5))
            qkvr, al_t = _inproj(x2, mod3, g_pre, w_main, jj, [0], (2 * kd + 2 * vd) // tn, epi,
                                 [BF16], tm=tm_in, tn=tn, seq_len=seq,
                                 small_t=_pad_cols(w_al, GATE_ROWS).T, name="gla_in")
            a = _gla_rec(qkvr.reshape(bsz, seq, -1), al_t, gla_w_alpha[jj],
                         gla_b_alpha[jj], gla_b_r[jj], gla_norm_g[jj], heads=heads, chunk=GLA_CHUNK)
            w_out = gla_w_out
        ffn_mod3 = mods[2 * i + 1].reshape(bsz, 1, 3 * d)
        x2, hn = _outproj(a.reshape(n, d), w_out.astype(BF16), jj, x2, mod3, g_post, tm=tm_out,
                          seq_len=seq, next_pre=(ffn_mod3, norm_pre[i, 1]), name="mixer_out")
        act = _ffn_in(hn, ffn_w_gate_up, i, tm=_tile(2048, n), tn=_tile(512, d_ff))
        x2 = _outproj(act, w_down, i, x2, ffn_mod3, norm_post[i, 1], tm=tm_out, seq_len=seq,
                      name="ffn_out")

    return x2.reshape(bsz, seq, d)
```

```python
import functools

import numpy as np
import jax
import jax.numpy as jnp
from jax import lax
from jax.experimental import pallas as pl
from jax.experimental.pallas import tpu as pltpu

F32 = jnp.float32
BF16 = jnp.bfloat16

RMS_EPS = 1e-6
LN_EPS = 1e-5
MLSTM_HEADS = 8
MLSTM_CHUNK = 256
GATE_SOFTCAP = 15.0
GLA_HEADS = 4
GLA_CHUNK = 256
GLA_TEMP = 16.0

LANE = 128
GATE_ROWS = 16
VMEM_LIMIT_BYTES = 56 * 1024 * 1024


def _tile(pref, dim):
    return pref if dim % pref == 0 else dim


def _params(*sem):
    return pltpu.CompilerParams(dimension_semantics=sem, vmem_limit_bytes=VMEM_LIMIT_BYTES)


def _sigmoid(z):
    return 1.0 / (1.0 + jnp.exp(-z))


def _silu(z):
    return z * _sigmoid(z)


def _log_sigmoid(z):
    return jnp.minimum(z, 0.0) - jnp.log(1.0 + jnp.exp(-jnp.abs(z)))


def _dot(a, b):
    return jnp.dot(a, b, preferred_element_type=F32)


def _dot_nt(a, b):
    return lax.dot_general(a, b, (((1,), (1,)), ((), ())), preferred_element_type=F32)


def _dot_tn(a, b):
    return lax.dot_general(a, b, (((0,), (0,)), ((), ())), preferred_element_type=F32)


def _split3(x):
    hi = x.astype(BF16)
    r1 = x - hi.astype(F32)
    mid = r1.astype(BF16)
    lo = (r1 - mid.astype(F32)).astype(BF16)
    return hi, mid, lo


def _tri(n, upper):
    r = lax.broadcasted_iota(jnp.int32, (n, n), 0)
    c = lax.broadcasted_iota(jnp.int32, (n, n), 1)
    return jnp.where((r <= c) if upper else (c <= r), 1.0, 0.0).astype(BF16)


def _cumsum_rows(x):
    tri = _tri(x.shape[0], upper=False)
    hi, mid, lo = _split3(x)
    return _dot(tri, hi) + _dot(tri, mid) + _dot(tri, lo)


def _cumsum_lanes(x):
    tri = _tri(x.shape[1], upper=True)
    hi, mid, lo = _split3(x)
    return _dot(hi, tri) + _dot(mid, tri) + _dot(lo, tri)


def _ada_body(c_ref, w_ref, b_ref, o_ref):
    sc = _silu(c_ref[...]).astype(BF16)
    o_ref[0] = _dot(sc, w_ref[0].astype(BF16)) + b_ref[0]


def _ada_mods(c, ada_w, ada_b):
    depth, two, d, d3 = ada_w.shape
    n_sub = depth * two
    bsz = c.shape[0]
    rows = -(-bsz // 16) * 16
    c_pad = jnp.pad(c, ((0, rows - bsz), (0, 0)))
    tn = _tile(1024, d3)
    out = pl.pallas_call(
        _ada_body,
        grid=(n_sub, d3 // tn),
        in_specs=[pl.BlockSpec((rows, d), lambda l, j: (0, 0)),
                  pl.BlockSpec((1, d, tn), lambda l, j: (l, 0, j)),
                  pl.BlockSpec((1, 1, tn), lambda l, j: (l, 0, j))],
        out_specs=pl.BlockSpec((1, rows, tn), lambda l, j: (l, 0, j)),
        out_shape=jax.ShapeDtypeStruct((n_sub, rows, d3), F32),
        compiler_params=_params("arbitrary", "arbitrary"),
        name="ada_mods",
    )(c_pad, ada_w.reshape(n_sub, d, d3), ada_b.reshape(n_sub, 1, d3))
    return out[:, :bsz, :]


def _inproj_body(*refs, n_parts, n_out, epi, with_small_t, row_chunk):
    x_ref, shift_ref, scale_ref, gpre_ref = refs[:4]
    w_refs = refs[4:4 + n_parts]
    pos = 4 + n_parts
    wst_ref = gt_ref = None
    if with_small_t:
        wst_ref = refs[pos]
        pos += 1
    out_refs = refs[pos:pos + n_out]
    pos += n_out
    if with_small_t:
        gt_ref = refs[pos]
        pos += 1
    hn_ref = refs[pos]

    j = pl.program_id(1)
    tm = x_ref.shape[0]

    @pl.when(j == 0)
    def _():
        gain = gpre_ref[...] * (1.0 + scale_ref[0])
        shift = shift_ref[0]

        def chunk(r, carry):
            rows = pl.ds(pl.multiple_of(r * row_chunk, row_chunk), row_chunk)
            x = x_ref[rows, :]
            h = x * lax.rsqrt(jnp.mean(x * x, axis=-1, keepdims=True) + RMS_EPS) * gain + shift
            hn_ref[rows, :] = h.astype(BF16)
            return carry
        lax.fori_loop(0, tm // row_chunk, chunk, 0, unroll=4)
        if with_small_t:
            gt_ref[...] = _dot_nt(wst_ref[...].astype(BF16), hn_ref[...])

    hn = hn_ref[...]
    accs = [_dot(hn, w_ref[...].astype(BF16)) for w_ref in w_refs]
    outs = epi(accs, j)
    for o_ref, o in zip(out_refs, outs):
        o_ref[...] = o.astype(o_ref.dtype)


def _inproj(x2, mod3, g_pre, w, layer, part_offsets, n_col_tiles, epi, out_defs, *, tm, tn, seq_len,
            small_t=None, name="inproj"):
    n, d = x2.shape
    tiles_per_seq = seq_len // tm
    n_parts = len(part_offsets)
    in_specs = [pl.BlockSpec((tm, d), lambda i, j: (i, 0)),
                pl.BlockSpec((1, 1, d), lambda i, j: (i // tiles_per_seq, 0, 0)),
                pl.BlockSpec((1, 1, d), lambda i, j: (i // tiles_per_seq, 0, 1)),
                pl.BlockSpec((1, d), lambda i, j: (0, 0))]
    args = [x2, mod3, mod3, g_pre.reshape(1, d)]
    for off in part_offsets:
        in_specs.append(pl.BlockSpec((None, d, tn), lambda i, j, off=off: (layer, 0, off + j)))
        args.append(w)
    if small_t is not None:
        in_specs.append(pl.BlockSpec((GATE_ROWS, d), lambda i, j: (0, 0)))
        args.append(small_t)
    out_specs = [pl.BlockSpec((tm, tn), lambda i, j: (i, j)) for _ in out_defs]
    out_shape = [jax.ShapeDtypeStruct((n, n_col_tiles * tn), dt) for dt in out_defs]
    if small_t is not None:
        out_specs.append(pl.BlockSpec((GATE_ROWS, tm), lambda i, j: (0, i)))
        out_shape.append(jax.ShapeDtypeStruct((GATE_ROWS, n), F32))
    body = functools.partial(_inproj_body, n_parts=n_parts, n_out=len(out_defs), epi=epi,
                             with_small_t=small_t is not None, row_chunk=min(32, tm))
    return pl.pallas_call(
        body,
        grid=(n // tm, n_col_tiles),
        in_specs=in_specs,
        out_specs=out_specs,
        out_shape=out_shape,
        scratch_shapes=[pltpu.VMEM((tm, d), BF16)],
        compiler_params=_params("arbitrary", "arbitrary"),
        name=name,
    )(*args)


def _outproj_body(*refs, row_chunk, with_next):
    if with_next:
        a_ref, w_ref, x_ref, gate_ref, gpost_ref, nshift_ref, nscale_ref, ngpre_ref, o_ref, hn_ref = refs
    else:
        a_ref, w_ref, x_ref, gate_ref, gpost_ref, o_ref = refs
    o_ref[...] = _dot(a_ref[...], w_ref[...])
    tm = o_ref.shape[0]

    gain = gpost_ref[...] * gate_ref[0]
    if with_next:
        next_gain = ngpre_ref[...] * (1.0 + nscale_ref[0])
        next_shift = nshift_ref[0]

    for r in range(tm // row_chunk):
        rows = slice(r * row_chunk, (r + 1) * row_chunk)
        y = o_ref[rows, :]
        xn = x_ref[rows, :] + y * lax.rsqrt(jnp.mean(y * y, axis=-1, keepdims=True) + RMS_EPS) * gain
        o_ref[rows, :] = xn
        if with_next:
            h = xn * lax.rsqrt(jnp.mean(xn * xn, axis=-1, keepdims=True) + RMS_EPS) * next_gain + next_shift
            hn_ref[rows, :] = h.astype(BF16)


def _outproj(a, w, layer, x2, mod3, g_post, *, tm, seq_len, next_pre=None, name="outproj"):
    n, kdim = a.shape
    d = w.shape[2]
    tiles_per_seq = seq_len // tm
    with_next = next_pre is not None
    body = functools.partial(_outproj_body, row_chunk=min(32, tm), with_next=with_next)
    in_specs = [pl.BlockSpec((tm, kdim), lambda i: (i, 0)),
                pl.BlockSpec((None, kdim, d), lambda i: (layer, 0, 0), pipeline_mode=pl.Buffered(1)),
                pl.BlockSpec((tm, d), lambda i: (i, 0)),
                pl.BlockSpec((1, 1, d), lambda i: (i // tiles_per_seq, 0, 2)),
                pl.BlockSpec((1, d), lambda i: (0, 0))]
    args = [a, w, x2, mod3, g_post.reshape(1, d)]
    out_specs = [pl.BlockSpec((tm, d), lambda i: (i, 0))]
    out_shape = [jax.ShapeDtypeStruct((n, d), F32)]
    if with_next:
        next_mod3, next_g_pre = next_pre
        in_specs += [pl.BlockSpec((1, 1, d), lambda i: (i // tiles_per_seq, 0, 0)),
                     pl.BlockSpec((1, 1, d), lambda i: (i // tiles_per_seq, 0, 1)),
                     pl.BlockSpec((1, d), lambda i: (0, 0))]
        args += [next_mod3, next_mod3, next_g_pre.reshape(1, d)]
        out_specs.append(pl.BlockSpec((tm, d), lambda i: (i, 0)))
        out_shape.append(jax.ShapeDtypeStruct((n, d), BF16))
    outs = pl.pallas_call(
        body,
        grid=(n // tm,),
        in_specs=in_specs,
        out_specs=out_specs,
        out_shape=out_shape,
        compiler_params=_params("arbitrary"),
        name=name,
    )(*args)
    return outs if with_next else outs[0]


def _ffn_in_body(hn_ref, wg_ref, wu_ref, o_ref):
    hn = hn_ref[...]
    gate = _dot(hn, wg_ref[...].astype(BF16))
    up = _dot(hn, wu_ref[...].astype(BF16))
    o_ref[...] = (_silu(gate) * up).astype(o_ref.dtype)


def _ffn_in(hn, w, layer, *, tm, tn):
    n, d = hn.shape
    d_ff = w.shape[2] // 2
    nj = d_ff // tn
    return pl.pallas_call(
        _ffn_in_body,
        grid=(nj, n // tm),
        in_specs=[pl.BlockSpec((tm, d), lambda j, i: (i, 0)),
                  pl.BlockSpec((None, d, tn), lambda j, i: (layer, 0, j)),
                  pl.BlockSpec((None, d, tn), lambda j, i: (layer, 0, nj + j))],
        out_specs=pl.BlockSpec((tm, tn), lambda j, i: (i, j)),
        out_shape=jax.ShapeDtypeStruct((n, d_ff), BF16),
        compiler_params=_params("arbitrary", "arbitrary"),
        name="ffn_in",
    )(hn, w, w)


def _dwconv_body(*refs, width, halo, col_chunk, row_tile, mode):
    if mode == "conformer":
        cur_ref, halo_ref, w_ref, b_ref, lng_ref, lnb_ref, o_ref, win_ref, sh_ref, y_ref = refs
    else:
        cur_ref, halo_ref, w_ref, gatein_ref, o_ref, win_ref, sh_ref, y_ref = refs
    t = pl.program_id(1)
    tt, d = y_ref.shape
    off = halo - (width - 1)
    sh_rows = sh_ref.shape[1]

    win_ref[0:halo, :] = jnp.where(t == 0, 0.0, halo_ref[0])
    win_ref[halo:halo + tt, :] = cur_ref[0]

    def col_body(c, carry):
        cs = pl.ds(pl.multiple_of(c * col_chunk, col_chunk), col_chunk)
        sh_ref[0] = win_ref[:, cs]
        for s in range(1, 8):
            sh_ref[s, 0:sh_rows - 8, :] = win_ref[pl.ds(s, sh_rows - 8), cs]
        for r in range(tt // row_tile):
            acc = jnp.zeros((row_tile, col_chunk), F32)
            for k in range(width):
                phase = (off + k) % 8
                base = off + k - phase
                acc = acc + w_ref[k:k + 1, cs] * sh_ref[phase, pl.ds(r * row_tile + base, row_tile), :]
            y_ref[pl.ds(r * row_tile, row_tile), cs] = acc
        return carry
    lax.fori_loop(0, d // col_chunk, col_body, 0)

    def tail(r, carry):
        rows = pl.ds(pl.multiple_of(r * 16, 16), 16)
        if mode == "conformer":
            u = y_ref[rows, :] + b_ref[...]
            mu = jnp.mean(u, axis=-1, keepdims=True)
            var = jnp.mean(jnp.square(u - mu), axis=-1, keepdims=True)
            un = (u - mu) * lax.rsqrt(var + LN_EPS) * lng_ref[...] + lnb_ref[...]
            o_ref[0, rows, :] = _silu(un).astype(o_ref.dtype)
        else:
            o_ref[0, rows, :] = (gatein_ref[0, rows, :] * y_ref[rows, :]).astype(o_ref.dtype)
        return carry
    lax.fori_loop(0, tt // 16, tail, 0, unroll=4)


def _dwconv(u3, w, *, mode, extra, tt):
    bsz, seq, d = u3.shape
    width = w.shape[0]
    halo = 8 * (-(-(width - 1) // 8))
    w_pad = jnp.pad(w, ((0, halo - width), (0, 0)))
    tiles = tt // halo
    in_specs = [pl.BlockSpec((1, tt, d), lambda b, t: (b, t, 0)),
                pl.BlockSpec((1, halo, d), lambda b, t: (b, jnp.maximum(t * tiles - 1, 0), 0)),
                pl.BlockSpec((halo, d), lambda b, t: (0, 0))]
    args = [u3, u3, w_pad]
    if mode == "conformer":
        for v in extra:
            in_specs.append(pl.BlockSpec((1, d), lambda b, t: (0, 0)))
            args.append(v.reshape(1, d))
    else:
        in_specs.append(pl.BlockSpec((1, tt, d), lambda b, t: (b, t, 0)))
        args.append(extra[0])
    col_chunk = min(256, d)
    body = functools.partial(_dwconv_body, width=width, halo=halo, col_chunk=col_chunk,
                             row_tile=min(64, tt), mode=mode)
    return pl.pallas_call(
        body,
        grid=(bsz, seq // tt),
        in_specs=in_specs,
        out_specs=pl.BlockSpec((1, tt, d), lambda b, t: (b, t, 0)),
        out_shape=jax.ShapeDtypeStruct((bsz, seq, d), BF16),
        scratch_shapes=[pltpu.VMEM((halo + tt, d), F32),
                        pltpu.VMEM((8, halo + tt, col_chunk), F32),
                        pltpu.VMEM((tt, d), F32)],
        compiler_params=_params("arbitrary", "arbitrary"),
        name="dwconv_" + mode,
    )(*args)


def _split3_f32(x):
    hi = x.astype(BF16).astype(F32)
    r1 = x - hi
    mid = r1.astype(BF16).astype(F32)
    lo = (r1 - mid).astype(BF16).astype(F32)
    return hi, mid, lo


def _mlstm_body(q_ref, k_ref, v_ref, o_ref, gt_ref, bcol_ref, ng_ref, out_ref, c_ref, m_ref, *, heads,
                head_group_size):
    c = pl.program_id(1)
    L = q_ref.shape[1]
    dk = q_ref.shape[2] // heads
    dv = v_ref.shape[2] // heads
    rep = dv // L

    @pl.when(c == 0)
    def _():
        c_ref[...] = jnp.zeros_like(c_ref)
        m_ref[...] = jnp.zeros_like(m_ref)

    g = gt_ref[...] + bcol_ref[...]
    g = GATE_SOFTCAP * jnp.tanh(g / GATE_SOFTCAP)
    cum = _cumsum_lanes(_log_sigmoid(g))
    li = g[0:heads]
    b = cum[heads:2 * heads]
    r = li - b
    lane = lax.broadcasted_iota(jnp.int32, (heads, L), 1)
    pm = r
    shift = 1
    while shift < L:
        pm = jnp.maximum(pm, jnp.where(lane >= shift, pltpu.roll(pm, shift, 1), -jnp.inf))
        shift *= 2
    m_prev = m_ref[...]
    u = jnp.maximum(m_prev, pm)
    u_last = jnp.broadcast_to(u[:, L - 1:L], (heads, L))
    b_last = jnp.broadcast_to(b[:, L - 1:L], (heads, L))
    m_ref[...] = b_last + u_last
    decay = jnp.exp(m_prev - u_last)
    lhs_terms = _split3_f32(-u) + _split3_f32(m_prev - u) + _split3_f32(-(b + u)) + _split3_f32(r - u_last)
    lhs_rows = (0, 1, 2, 6, 7, 8, 9, 10, 11, 12, 13, 14)
    r_terms = _split3_f32(r)

    rid = lax.broadcasted_iota(jnp.int32, (GATE_ROWS, L), 0)
    lhs_const = jnp.where((rid >= 3) & (rid < 6), 1.0, 0.0)
    rhs_blocks = [jnp.where((rid >= lo) & (rid < lo + 3), 1.0, 0.0).astype(BF16) for lo in (6, 9, 12)]
    rhs0_const = jnp.where(rid < 3, 1.0, 0.0)

    row = lax.broadcasted_iota(jnp.int32, (L, L), 0)
    col = lax.broadcasted_iota(jnp.int32, (L, L), 1)
    causal = col <= row
    ones_ll = jnp.ones((L, L), BF16)
    ones_dv = jnp.ones((dv, L), BF16)

    def operands(h):
        lhs = lhs_const
        for i, term in zip(lhs_rows, lhs_terms):
            lhs = jnp.where(rid == i, term[h:h + 1, :], lhs)
        rhs0 = rhs0_const
        for i, term in zip((3, 4, 5), r_terms):
            rhs0 = jnp.where(rid == i, term[h:h + 1, :], rhs0)
        return lhs.astype(BF16), jnp.concatenate([rhs0.astype(BF16)] + rhs_blocks, axis=1)

    def mean_sq(x):
        xx = x * x
        hi = xx.astype(BF16)
        lo = (xx - hi.astype(F32)).astype(BF16)
        return (_dot(hi, ones_dv) + _dot(lo, ones_dv)) * (1.0 / dv)

    def head_group(H):
        qs = {h: q_ref[0, :, h * dk:(h + 1) * dk] for h in H}
        ks = {h: k_ref[0, :, h * dk:(h + 1) * dk] for h in H}
        vs = {h: v_ref[0, :, h * dv:(h + 1) * dv] for h in H}
        caugs = {h: c_ref[h] for h in H}
        zs = {h: _dot_tn(*operands(h)) for h in H}
        qks = {h: _dot_nt(qs[h], ks[h]) for h in H}
        qcs = {h: _dot(qs[h], caugs[h].astype(BF16)) for h in H}
        ps = {h: jnp.exp(jnp.where(causal, zs[h][:, 0:L], -jnp.inf)) for h in H}
        ss = {h: (qks[h] * ps[h]).astype(BF16) for h in H}
        svs = {h: _dot(ss[h], jnp.concatenate([vs[h], ones_ll], axis=1)) for h in H}
        nds = {h: svs[h] + jnp.concatenate([jnp.exp(zs[h][:, L:2 * L])] * (rep + 1), axis=1) * qcs[h]
               for h in H}
        scales = {h: 1.0 / jnp.maximum(jnp.abs(nds[h][:, dv:]), jnp.exp(zs[h][:, 2 * L:3 * L])) for h in H}
        houts = {h: nds[h][:, :dv] * jnp.concatenate([scales[h]] * rep, axis=1) for h in H}

        for h in H:
            wk = jnp.exp(zs[h][:, 3 * L:4 * L])
            vw = jnp.concatenate([vs[h].astype(F32) * jnp.concatenate([wk] * rep, axis=1), wk], axis=1)
            dec = jnp.concatenate([jnp.broadcast_to(decay[h:h + 1, :], (dk, L))] * (rep + 1), axis=1)
            c_ref[h] = dec * caugs[h] + _dot_tn(ks[h], vw.astype(BF16))

        mss = {h: mean_sq(houts[h]) for h in H}
        for h in H:
            hn = houts[h] * jnp.concatenate([lax.rsqrt(mss[h] + RMS_EPS)] * rep, axis=1)
            hn = hn * ng_ref[:, h * dv:(h + 1) * dv]
            og = _sigmoid(o_ref[0, :, h * dv:(h + 1) * dv].astype(F32))
            out_ref[0, :, h * dv:(h + 1) * dv] = (og * hn).astype(out_ref.dtype)

    for g0 in range(0, heads, head_group_size):
        head_group(range(g0, min(g0 + head_group_size, heads)))


def _mlstm_rec(qkvo3, gt, b_if, norm_g, *, heads, chunk):
    bsz, seq, _ = qkvo3.shape
    dvt = norm_g.shape[0]
    dkt = dvt // 2
    nc = seq // chunk
    assert (dvt // heads) % chunk == 0 and 2 * heads <= GATE_ROWS
    bcol = jnp.pad(b_if, (0, GATE_ROWS - b_if.shape[0])).reshape(GATE_ROWS, 1)
    body = functools.partial(_mlstm_body, heads=heads, head_group_size=2)
    return pl.pallas_call(
        body,
        grid=(bsz, nc),
        in_specs=[pl.BlockSpec((1, chunk, dkt), lambda b, c: (b, c, 0)),
                  pl.BlockSpec((1, chunk, dkt), lambda b, c: (b, c, 1)),
                  pl.BlockSpec((1, chunk, dvt), lambda b, c: (b, c, 1)),
                  pl.BlockSpec((1, chunk, dvt), lambda b, c: (b, c, 2)),
                  pl.BlockSpec((GATE_ROWS, chunk), lambda b, c: (0, b * nc + c)),
                  pl.BlockSpec((GATE_ROWS, 1), lambda b, c: (0, 0)),
                  pl.BlockSpec((1, dvt), lambda b, c: (0, 0))],
        out_specs=pl.BlockSpec((1, chunk, dvt), lambda b, c: (b, c, 0)),
        out_shape=jax.ShapeDtypeStruct((bsz, seq, dvt), BF16),
        scratch_shapes=[pltpu.VMEM((heads, dkt // heads, dvt // heads + chunk), F32),
                        pltpu.VMEM((heads, chunk), F32)],
        compiler_params=_params("arbitrary", "arbitrary"),
        name="mlstm_rec",
    )(qkvo3, qkvo3, qkvo3, qkvo3, gt, bcol, norm_g.reshape(1, dvt))


def _gla_level_masks(chunk):
    t = np.arange(chunk)[:, None]
    s = np.arange(chunk)[None, :]
    masks = []
    b = chunk // 2
    while b >= 1:
        masks.append((t // (2 * b) == s // (2 * b)) & ((t % (2 * b)) >= b) & ((s % (2 * b)) < b))
        b //= 2
    return jnp.asarray(np.stack(masks), dtype=F32)


def _mid_rows(bc, b, row):
    L, dk = bc.shape
    if b >= 4:
        return jnp.concatenate([jnp.broadcast_to(bc[m:m + 1, :], (2 * b, dk)) for m in range(b, L, 2 * b)],
                               axis=0)
    up1 = pltpu.roll(bc, L - 1, 0)
    if b == 1:
        return jnp.where((row & 1) == 0, up1, bc)
    up2 = pltpu.roll(bc, L - 2, 0)
    down1 = pltpu.roll(bc, 1, 0)
    r4 = row & 3
    return jnp.where(r4 == 0, up2, jnp.where(r4 == 1, up1, jnp.where(r4 == 2, bc, down1)))


def _gla_body(q_ref, k_ref, v_ref, r_ref, al_ref, wa_ref, ba_ref, br_ref, ng_ref, mask_ref,
              out_ref, st_ref, *, heads, head_group_size):
    c = pl.program_id(1)
    L = q_ref.shape[1]
    dk = q_ref.shape[2] // heads
    dv = v_ref.shape[2] // heads
    n_levels = mask_ref.shape[0]

    @pl.when(c == 0)
    def _():
        st_ref[...] = jnp.zeros_like(st_ref)

    z = _dot_tn(al_ref[...].astype(BF16), wa_ref[...]) + ba_ref[...]
    log_alpha = _log_sigmoid(z) / GLA_TEMP
    bc_all = _cumsum_rows(log_alpha)

    eye = (lax.broadcasted_iota(jnp.int32, (L, L), 0) == lax.broadcasted_iota(jnp.int32, (L, L), 1))
    row = lax.broadcasted_iota(jnp.int32, (L, dk), 0)

    def head_group(H):
        qfs = {h: q_ref[0, :, h * dk:(h + 1) * dk].astype(F32) for h in H}
        kfs = {h: k_ref[0, :, h * dk:(h + 1) * dk].astype(F32) for h in H}
        vs = {h: v_ref[0, :, h * dv:(h + 1) * dv] for h in H}
        bcs = {h: bc_all[:, h * dk:(h + 1) * dk] for h in H}
        sts = {h: st_ref[h] for h in H}

        os_ = {h: _dot_nt((qfs[h] * jnp.exp(bcs[h])).astype(BF16), sts[h].astype(BF16)) for h in H}

        accs = {h: jnp.where(eye, jnp.sum(qfs[h] * kfs[h], axis=-1, keepdims=True), 0.0) for h in H}
        for i in range(n_levels):
            b = L >> (i + 1)
            upper = (row & (2 * b - 1)) >= b
            cls = {h: (jnp.where(upper, qfs[h], kfs[h])
                       * jnp.exp(-jnp.abs(bcs[h] - _mid_rows(bcs[h], b, row)))).astype(BF16) for h in H}
            accs = {h: accs[h] + mask_ref[i] * _dot_nt(cls[h], cls[h]) for h in H}
        os_ = {h: os_[h] + _dot(accs[h].astype(BF16), vs[h]) for h in H}

        for h in H:
            b_last = bcs[h][L - 1:L, :]
            k_up = (kfs[h] * jnp.exp(jnp.minimum(b_last - bcs[h], 0.0))).astype(BF16)
            st_ref[h] = sts[h] * jnp.exp(b_last) + _dot_tn(vs[h], k_up)

        for h in H:
            o = os_[h]
            on = o * lax.rsqrt(jnp.mean(o * o, axis=-1, keepdims=True) + RMS_EPS)
            on = on * ng_ref[:, h * dv:(h + 1) * dv]
            rg = r_ref[0, :, h * dv:(h + 1) * dv].astype(F32) + br_ref[:, h * dv:(h + 1) * dv]
            out_ref[0, :, h * dv:(h + 1) * dv] = (on * _silu(rg)).astype(out_ref.dtype)

    for g0 in range(0, heads, head_group_size):
        head_group(range(g0, min(g0 + head_group_size, heads)))


def _gla_rec(qkvr3, al_t, w_alpha, b_alpha, b_r, norm_g, *, heads, chunk):
    bsz, seq, _ = qkvr3.shape
    dvt = norm_g.shape[0]
    dkt = dvt // 2
    nc = seq // chunk
    rank = w_alpha.shape[0]
    assert rank <= GATE_ROWS and chunk % LANE == 0
    wa = jnp.pad(w_alpha, ((0, GATE_ROWS - rank), (0, 0))).astype(BF16)
    masks = _gla_level_masks(chunk)
    body = functools.partial(_gla_body, heads=heads, head_group_size=4)
    return pl.pallas_call(
        body,
        grid=(bsz, seq // chunk),
        in_specs=[pl.BlockSpec((1, chunk, dkt), lambda b, c: (b, c, 0)),
                  pl.BlockSpec((1, chunk, dkt), lambda b, c: (b, c, 1)),
                  pl.BlockSpec((1, chunk, dvt), lambda b, c: (b, c, 1)),
                  pl.BlockSpec((1, chunk, dvt), lambda b, c: (b, c, 2)),
                  pl.BlockSpec((GATE_ROWS, chunk), lambda b, c: (0, b * nc + c)),
                  pl.BlockSpec((GATE_ROWS, dkt), lambda b, c: (0, 0)),
                  pl.BlockSpec((1, dkt), lambda b, c: (0, 0)),
                  pl.BlockSpec((1, dvt), lambda b, c: (0, 0)),
                  pl.BlockSpec((1, dvt), lambda b, c: (0, 0)),
                  pl.BlockSpec(masks.shape, lambda b, c: (0, 0, 0))],
        out_specs=pl.BlockSpec((1, chunk, dvt), lambda b, c: (b, c, 0)),
        out_shape=jax.ShapeDtypeStruct((bsz, seq, dvt), BF16),
        scratch_shapes=[pltpu.VMEM((heads, dvt // heads, dkt // heads), F32)],
        compiler_params=_params("arbitrary", "arbitrary"),
        name="gla_rec",
    )(qkvr3, qkvr3, qkvr3, qkvr3, al_t, wa, b_alpha.reshape(1, dkt), b_r.reshape(1, dvt),
      norm_g.reshape(1, dvt), masks)


def _epi_glu(accs, j):
    a, g = accs
    return [a * _sigmoid(g)]


def _epi_sconv(accs, j):
    bg, cg, xv = accs
    return [bg, cg * xv]


def _epi_scale_first(accs, j, *, n_first, scale):
    (a,) = accs
    return [a * jnp.where(j < n_first, scale, 1.0)]


def _pad_cols(w, cols):
    return jnp.pad(w, ((0, 0), (0, cols - w.shape[1])))


def kernel(x, c, conf_w_in, conf_w_dw, conf_b_dw, conf_ln_g, conf_ln_b, conf_w_out, sconv_w_in, sconv_w_conv, sconv_w_out, mlstm_w_in, mlstm_b_if, mlstm_norm_g, mlstm_w_out, gla_w_in, gla_w_alpha, gla_b_alpha, gla_b_r, gla_norm_g, gla_w_out, ada_w, ada_b, norm_pre, norm_post, ffn_w_gate_up, ffn_w_down):
    bsz, seq, d = x.shape
    n = bsz * seq
    depth = ada_w.shape[0]
    d_ff = ffn_w_down.shape[1]
    tm_in = _tile(1024, seq)
    tm_out = _tile(512, seq)
    tt = _tile(512, seq)

    mods = _ada_mods(c, ada_w, ada_b)
    x2 = x.reshape(n, d)
    w_down = ffn_w_down.astype(BF16)

    for i in range(depth):
        kind, jj = i % 4, i // 4
        mod3 = mods[2 * i].reshape(bsz, 1, 3 * d)
        g_pre, g_post = norm_pre[i, 0], norm_post[i, 0]
        if kind == 0:
            tn = _tile(512, d)
            (u,) = _inproj(x2, mod3, g_pre, conf_w_in, jj, [0, d // tn], d // tn, _epi_glu,
                           [F32], tm=tm_in, tn=tn, seq_len=seq, name="conf_in")
            a = _dwconv(u.reshape(bsz, seq, d), conf_w_dw[jj], mode="conformer",
                        extra=(conf_b_dw[jj], conf_ln_g[jj], conf_ln_b[jj]), tt=tt)
            w_out = conf_w_out
        elif kind == 1:
            tn = _tile(512, d)
            bg, p = _inproj(x2, mod3, g_pre, sconv_w_in.astype(BF16), jj, [0, d // tn, 2 * d // tn],
                            d // tn, _epi_sconv, [F32, F32], tm=tm_in, tn=tn, seq_len=seq, name="sconv_in")
            a = _dwconv(p.reshape(bsz, seq, d), sconv_w_conv[jj], mode="sconv",
                        extra=(bg.reshape(bsz, seq, d),), tt=tt)
            w_out = sconv_w_out
        elif kind == 2:
            heads = MLSTM_HEADS
            qk, vd = d // 2, d
            w_main = mlstm_w_in[:, :, :2 * qk + 2 * vd].astype(BF16)
            w_if = mlstm_w_in[jj, :, 2 * qk + 2 * vd:]
            tn = _tile(1024, qk)
            epi = functools.partial(_epi_scale_first, n_first=qk // tn, scale=float((qk // heads) ** -0.5))
            qkvo, gt = _inproj(x2, mod3, g_pre, w_main, jj, [0], (2 * qk + 2 * vd) // tn, epi,
                               [BF16], tm=tm_in, tn=tn, seq_len=seq,
                               small_t=_pad_cols(w_if, GATE_ROWS).T, name="mlstm_in")
            a = _mlstm_rec(qkvo.reshape(bsz, seq, -1), gt, mlstm_b_if[jj], mlstm_norm_g[jj], heads=heads,
                           chunk=MLSTM_CHUNK)
            w_out = mlstm_w_out
        else:
            heads = GLA_HEADS
            kd, vd = d // 2, d
            w_main = gla_w_in[:, :, :2 * kd + 2 * vd].astype(BF16)
            w_al = gla_w_in[jj, :, 2 * kd + 2 * vd:]
            tn = _tile(1024, kd)
            epi = functools.partial(_epi_scale_first, n_first=kd // tn, scale=float((kd // heads) ** -0.5))
            qkvr, al_t = _inproj(x2, mod3, g_pre, w_main, jj, [0], (2 * kd + 2 * vd) // tn, epi,
                                 [BF16], tm=tm_in, tn=tn, seq_len=seq,
                                 small_t=_pad_cols(w_al, GATE_ROWS).T, name="gla_in")
            a = _gla_rec(qkvr.reshape(bsz, seq, -1), al_t, gla_w_alpha[jj],
                         gla_b_alpha[jj], gla_b_r[jj], gla_norm_g[jj], heads=heads, chunk=GLA_CHUNK)
            w_out = gla_w_out
        ffn_mod3 = mods[2 * i + 1].reshape(bsz, 1, 3 * d)
        x2, hn = _outproj(a.reshape(n, d), w_out.astype(BF16), jj, x2, mod3, g_post, tm=tm_out,
                          seq_len=seq, next_pre=(ffn_mod3, norm_pre[i, 1]), name="mixer_out")
        act = _ffn_in(hn, ffn_w_gate_up, i, tm=_tile(2048, n), tn=_tile(512, d_ff))
        x2 = _outproj(act, w_down, i, x2, ffn_mod3, norm_post[i, 1], tm=tm_out, seq_len=seq,
                      name="ffn_out")

    return x2.reshape(bsz, seq, d)
```
